```python
import jax, jax.numpy as jnp
from jax import lax
import numpy as np

D_MODEL = 1024
BATCH = 32
SEQ = 2048
DEPTH = 2
DEC_BATCH = 16
DEC_SEQ = 64
PAST_LEN = 2048

CHUNK = 64
Q_BLOCK = 128
BR_W = D_MODEL // 2
ML_HEADS = 4
ML_DV = BR_W // ML_HEADS
ML_DK = ML_DV // 2
ML_W = ML_HEADS * ML_DV
FX_HD = 64
FX_HEADS = BR_W // FX_HD
FX_W = FX_HEADS * FX_HD
HG_HEADS = 4
HG_DK = BR_W // HG_HEADS
HG_DV = HG_DK
HG_W = HG_HEADS * HG_DK
HG_BLOCK = 16
RG_W = BR_W
RG_BLOCKS = 8
RG_BD = RG_W // RG_BLOCKS
RG_CONV = 4
RG_C = 8.0
D_FF = 2 * D_MODEL
FFN_CONV = 3
N_BRANCH = 4
DN_ALPHA = (2 * DEPTH) ** 0.25
DN_BETA = (8 * DEPTH) ** -0.25
PROJ_SIZES = (ML_HEADS * ML_DK, ML_HEADS * ML_DK, ML_W, ML_W, ML_HEADS, ML_HEADS,
              FX_W, FX_W, FX_W, FX_HEADS,
              HG_W, HG_W, HG_W, HG_W,
              RG_W, RG_W)
P_IN = sum(PROJ_SIZES)
ML_F_OFFSET = sum(PROJ_SIZES[:5])

NEG_BIG = -1e30
TINY = 1e-30

kernel_name = 'hybrid_streaming_encoder_step'

F32 = jnp.float32


def layer_norm(x, g, b, eps=1e-5):
    xf = x.astype(F32)
    mu = xf.mean(-1, keepdims=True)
    var = jnp.mean(jnp.square(xf - mu), -1, keepdims=True)
    return ((xf - mu) * lax.rsqrt(var + eps) * g.astype(F32) + b.astype(F32)).astype(x.dtype)


def head_layer_norm(h, g, eps=1e-5):
    hf = h.astype(F32)
    mu = hf.mean(-1, keepdims=True)
    var = jnp.mean(jnp.square(hf - mu), -1, keepdims=True)
    y = (hf - mu) * lax.rsqrt(var + eps)
    return y.reshape(*h.shape[:-2], -1) * g.astype(F32)


def head_rms_norm(h, g, eps=1e-6):
    hf = h.astype(F32)
    y = hf * lax.rsqrt(jnp.mean(jnp.square(hf), -1, keepdims=True) + eps)
    return y.reshape(*h.shape[:-2], -1) * g.astype(F32)


def causal_dwconv(u, buf, w, b):
    width = w.shape[0]
    t = u.shape[1]
    full = jnp.concatenate([buf.astype(u.dtype), u], axis=1)
    y = b + full[:, 0:t] * w[0]
    for j in range(1, width):
        y = y + full[:, j:j + t] * w[j]
    return y, full[:, t:]


def _pad_time(a, pad, value=0.0):
    if pad == 0:
        return a
    widths = [(0, 0)] * a.ndim
    widths[1] = (0, pad)
    return jnp.pad(a, widths, constant_values=value)


def _to_blocks(a, block):
    b, t = a.shape[:2]
    a = a.reshape(b, t // block, block, *a.shape[2:])
    return jnp.moveaxis(jnp.moveaxis(a, 1, 0), 2, 3)


def _from_blocks(a):
    n, b, h, l = a.shape[:4]
    a = jnp.moveaxis(jnp.moveaxis(a, 0, 1), 2, 3)
    return a.reshape(b, n * l, h, *a.shape[4:])


def mlstm_chunkwise(q, k, v, i_pre, logf, c0, n0, m0):
    t_len = q.shape[1]
    nb = -(-t_len // CHUNK)
    pad = nb * CHUNK - t_len
    qb = _to_blocks(_pad_time(q.astype(F32), pad), CHUNK)
    kb = _to_blocks(_pad_time(k.astype(F32), pad), CHUNK)
    vb = _to_blocks(_pad_time(v.astype(F32), pad), CHUNK)
    ib = _to_blocks(_pad_time(i_pre.astype(F32), pad, NEG_BIG), CHUNK)
    fb = _to_blocks(_pad_time(logf.astype(F32), pad), CHUNK)
    causal = jnp.tril(jnp.ones((CHUNK, CHUNK), dtype=bool))

    def step(carry, blk):
        c, n, m = carry
        qc, kc, vc, ic, fc = blk
        fcum = jnp.cumsum(fc, axis=-1)
        d = jnp.where(causal, fcum[..., :, None] - fcum[..., None, :] + ic[..., None, :], NEG_BIG)
        prev = fcum + m[..., None]
        m_t = jnp.maximum(prev, d.max(-1))
        w = jnp.exp(d - m_t[..., None]) * jnp.einsum('bhtk,bhsk->bhts', qc, kc)
        g = jnp.exp(prev - m_t)
        num = jnp.einsum('bhts,bhsv->bhtv', w, vc) + g[..., None] * jnp.einsum('bhtk,bhkv->bhtv', qc, c)
        den = w.sum(-1) + g * jnp.einsum('bhtk,bhk->bht', qc, n)
        h = num / jnp.maximum(jnp.abs(den), jnp.exp(-m_t))[..., None]
        w_last = jnp.exp(d[..., -1, :] - m_t[..., -1:])
        g_last = g[..., -1]
        c_new = g_last[..., None, None] * c + jnp.einsum('bhs,bhsk,bhsv->bhkv', w_last, kc, vc)
        n_new = g_last[..., None] * n + jnp.einsum('bhs,bhsk->bhk', w_last, kc)
        return (c_new, n_new, m_t[..., -1]), h

    (c, n, m), h = lax.scan(step, (c0.astype(F32), n0.astype(F32), m0.astype(F32)), (qb, kb, vb, ib, fb))
    return _from_blocks(h)[:, :t_len], c, n, m


def hgrn2_chunkwise(q, k, v, logf, s0):
    t_len = q.shape[1]
    nb = -(-t_len // HG_BLOCK)
    pad = nb * HG_BLOCK - t_len
    qb, kb, vb, fb = (_to_blocks(_pad_time(a.astype(F32), pad), HG_BLOCK) for a in (q, k, v, logf))
    causal = jnp.tril(jnp.ones((HG_BLOCK, HG_BLOCK), dtype=bool))

    def step(s, blk):
        qc, kc, vc, fc = blk
        bcum = jnp.cumsum(fc, axis=2)
        qe = qc * jnp.exp(bcum)
        a = jnp.where(causal, jnp.einsum('bhtk,bhsk->bhts', qe, kc * jnp.exp(-bcum)), 0.0)
        o = jnp.einsum('bhts,bhsv->bhtv', a, vc) + jnp.einsum('bhtk,bhkv->bhtv', qe, s)
        b_last = bcum[:, :, -1]
        s_new = jnp.exp(b_last)[..., None] * s + jnp.einsum(
            'bhsk,bhsv->bhkv', kc * jnp.exp(b_last[:, :, None] - bcum), vc)
        return s_new, o

    s, o = lax.scan(step, s0.astype(F32), (qb, kb, vb, fb))
    return _from_blocks(o)[:, :t_len], s


def rglru(u, h0, w_a, b_a, w_x, b_x, lam):
    b, t_len, w = u.shape
    uf = u.astype(F32)
    ub = uf.reshape(b, t_len, RG_BLOCKS, RG_BD)
    r = jax.nn.sigmoid(jnp.einsum('btnd,nde->btne', ub, w_a.astype(F32)).reshape(b, t_len, w) + b_a)
    ig = jax.nn.sigmoid(jnp.einsum('btnd,nde->btne', ub, w_x.astype(F32)).reshape(b, t_len, w) + b_x)
    log_a = -RG_C * r * jax.nn.softplus(-lam.astype(F32))
    a = jnp.exp(log_a)
    bterm = jnp.sqrt(jnp.maximum(-jnp.expm1(2.0 * log_a), 0.0)) * (ig * uf)
    bterm = bterm.at[:, 0].add(a[:, 0] * h0.astype(F32))

    def combine(e1, e2):
        a1, b1 = e1
        a2, b2 = e2
        return a1 * a2, a2 * b1 + b2

    _, h = lax.associative_scan(combine, (a, bterm), axis=1)
    return h, h[:, -1]


def fox_attend(q, k, v, cq, ck, q_start):
    tq, tk = q.shape[1], k.shape[1]
    s = jnp.einsum('bqhd,bkhd->bhqk', q, k).astype(F32) * FX_HD ** -0.5
    s = s + jnp.swapaxes(cq, 1, 2)[..., :, None] - jnp.swapaxes(ck, 1, 2)[..., None, :]
    mask = jnp.arange(tk)[None, :] <= (q_start + jnp.arange(tq))[:, None]
    p = jax.nn.softmax(jnp.where(mask, s, NEG_BIG), axis=-1)
    return jnp.einsum('bhqk,bkhd->bqhd', p.astype(v.dtype), v)


def hgrn_lower_bounds(logits):
    p = jax.nn.softmax(logits.astype(F32), axis=0)
    return jnp.cumsum(p, axis=0) - p[0]


def trunk_layer(x, fox_past, ml_c, ml_n, ml_m, hg_s, rg_h, rg_buf, ff_buf, lb,
                w_in, b_in, ml_norm_g, hg_norm_g, rg_conv_w, rg_conv_b, rg_w_a, rg_b_a,
                rg_w_x, rg_b_x, rg_lambda, w_mg, b_mg, w_br, w_out, ln1_g, ln1_b,
                w_ff_gate, w_ff_up, ff_conv_w, ff_conv_b, w_ff_down, ln2_g, ln2_b):
    bsz, t_len, _ = x.shape
    proj = jnp.einsum('btd,dp->btp', x, w_in) + b_in
    (ml_q, ml_k, ml_v, ml_o, ml_i, ml_f,
     fx_q, fx_k, fx_v, fx_f,
     hg_f, hg_i, hg_q, hg_g,
     rg_x, rg_g) = jnp.split(proj, np.cumsum(PROJ_SIZES)[:-1].tolist(), axis=-1)

    h_ml, ml_c, ml_n, ml_m = mlstm_chunkwise(
        ml_q.reshape(bsz, t_len, ML_HEADS, ML_DK),
        ml_k.reshape(bsz, t_len, ML_HEADS, ML_DK) * ML_DK ** -0.5,
        ml_v.reshape(bsz, t_len, ML_HEADS, ML_DV),
        ml_i, jax.nn.log_sigmoid(ml_f.astype(F32)), ml_c, ml_n, ml_m)
    y_ml = jax.nn.sigmoid(ml_o) * head_layer_norm(h_ml, ml_norm_g).astype(x.dtype)

    fq = fx_q.reshape(bsz, t_len, FX_HEADS, FX_HD)
    fk = fx_k.reshape(bsz, t_len, FX_HEADS, FX_HD)
    fv = fx_v.reshape(bsz, t_len, FX_HEADS, FX_HD)
    f_log = jax.nn.log_sigmoid(fx_f.astype(F32))
    if fox_past is None:
        c = jnp.cumsum(f_log, axis=1)
        outs = []
        for s0 in range(0, t_len, Q_BLOCK):
            e = min(s0 + Q_BLOCK, t_len)
            outs.append(fox_attend(fq[:, s0:e], fk[:, :e], fv[:, :e], c[:, s0:e], c[:, :e], s0))
        o_fx = jnp.concatenate(outs, axis=1)
    else:
        k_past, v_past, logf_past = fox_past
        t_past = k_past.shape[1]
        k_all = jnp.concatenate([k_past.astype(fk.dtype), fk], axis=1)
        v_all = jnp.concatenate([v_past.astype(fv.dtype), fv], axis=1)
        c_all = jnp.cumsum(jnp.concatenate([logf_past.astype(F32), f_log], axis=1), axis=1)
        o_fx = fox_attend(fq, k_all, v_all, c_all[:, t_past:], c_all, t_past)
    y_fx = o_fx.reshape(bsz, t_len, FX_W)

    z = hg_f.astype(F32).reshape(bsz, t_len, HG_HEADS, HG_DK)
    lbh = lb.reshape(HG_HEADS, HG_DK)
    hg_logf = jnp.log(jnp.maximum(lbh + (1.0 - lbh) * jax.nn.sigmoid(z), TINY))
    hg_k = (1.0 - lbh) * jax.nn.sigmoid(-z)
    o_hg, hg_s = hgrn2_chunkwise(jax.nn.silu(hg_q).reshape(bsz, t_len, HG_HEADS, HG_DK), hg_k,
                                 hg_i.reshape(bsz, t_len, HG_HEADS, HG_DV), hg_logf, hg_s)
    y_hg = head_rms_norm(o_hg, hg_norm_g).astype(x.dtype) * jax.nn.silu(hg_g)

    u, rg_buf = causal_dwconv(rg_x, rg_buf, rg_conv_w, rg_conv_b)
    h_rg, rg_h = rglru(u, rg_h, rg_w_a, rg_b_a, rg_w_x, rg_b_x, rg_lambda)
    y_rg = h_rg.astype(x.dtype) * jax.nn.gelu(rg_g)

    terms = []
    for m_idx, y_b in enumerate((y_ml, y_fx, y_hg, y_rg)):
        gate = jax.nn.sigmoid(jnp.einsum('btd,de->bte', x, w_mg[m_idx]) + b_mg[m_idx])
        terms.append(gate * jnp.einsum('btw,wd->btd', y_b, w_br[m_idx]))
    mix = terms[0] + terms[1] + terms[2] + terms[3]
    x1 = layer_norm(DN_ALPHA * x + jnp.einsum('btd,de->bte', mix, w_out), ln1_g, ln1_b)

    gate_pre = jnp.einsum('btd,df->btf', x1, w_ff_gate)
    up = jnp.einsum('btd,df->btf', x1, w_ff_up)
    gate_c, ff_buf = causal_dwconv(gate_pre, ff_buf, ff_conv_w, ff_conv_b)
    ffn = jnp.einsum('btf,fd->btd', jax.nn.gelu(gate_c) * up, w_ff_down)
    x2 = layer_norm(DN_ALPHA * x1 + ffn, ln2_g, ln2_b)
    return x2, (fk, fv, f_log, ml_c, ml_n, ml_m, hg_s, rg_h, rg_buf, ff_buf)


def setup_inputs(seed: int = 0) -> dict:
    key = jax.random.key(seed)
    ks = iter(jax.random.split(key, 48))

    def nrm(shape, scale=1.0):
        return scale * jax.random.normal(next(ks), shape, jnp.float32)

    x_prompt = nrm((BATCH, SEQ, D_MODEL))
    x_sample = nrm((DEC_BATCH, DEC_SEQ, D_MODEL))
    cache_fox_k = nrm((DEPTH, DEC_BATCH, PAST_LEN, FX_HEADS, FX_HD))
    cache_fox_v = nrm((DEPTH, DEC_BATCH, PAST_LEN, FX_HEADS, FX_HD))
    cache_fox_logf = jax.nn.log_sigmoid(nrm((DEPTH, DEC_BATCH, PAST_LEN, FX_HEADS)) + 1.0)
    state_mlstm_c = nrm((DEPTH, DEC_BATCH, ML_HEADS, ML_DK, ML_DV), 0.1)
    state_mlstm_n = nrm((DEPTH, DEC_BATCH, ML_HEADS, ML_DK), 0.1)
    state_mlstm_m = nrm((DEPTH, DEC_BATCH, ML_HEADS))
    state_hgrn_s = nrm((DEPTH, DEC_BATCH, HG_HEADS, HG_DK, HG_DV))
    state_rglru_h = nrm((DEPTH, DEC_BATCH, RG_W))
    state_rglru_conv = nrm((DEPTH, DEC_BATCH, RG_CONV - 1, RG_W))
    state_ffn_conv = nrm((DEPTH, DEC_BATCH, FFN_CONV - 1, D_FF))

    w_in = nrm((DEPTH, D_MODEL, P_IN), D_MODEL ** -0.5)
    b_in = nrm((DEPTH, P_IN), 0.02).at[:, ML_F_OFFSET:ML_F_OFFSET + ML_HEADS].add(
        jnp.linspace(3.0, 6.0, ML_HEADS))
    ml_norm_g = 1.0 + nrm((DEPTH, ML_W), 0.02)
    hg_norm_g = 1.0 + nrm((DEPTH, HG_W), 0.02)
    hg_lb_logits = nrm((DEPTH, HG_W), 0.5)
    rg_conv_w = nrm((DEPTH, RG_CONV, RG_W), RG_CONV ** -0.5)
    rg_conv_b = nrm((DEPTH, RG_W), 0.02)
    rg_w_a = nrm((DEPTH, RG_BLOCKS, RG_BD, RG_BD), RG_BD ** -0.5)
    rg_b_a = nrm((DEPTH, RG_W), 0.02)
    rg_w_x = nrm((DEPTH, RG_BLOCKS, RG_BD, RG_BD), RG_BD ** -0.5)
    rg_b_x = nrm((DEPTH, RG_W), 0.02)
    a0 = jax.random.uniform(next(ks), (DEPTH, RG_W), jnp.float32, minval=0.9, maxval=0.999)
    rg_lambda = jnp.log(a0) - jnp.log1p(-a0)
    w_mg = nrm((DEPTH, N_BRANCH, D_MODEL, D_MODEL), D_MODEL ** -0.5)
    b_mg = nrm((DEPTH, N_BRANCH, D_MODEL), 0.02)
    w_br = nrm((DEPTH, N_BRANCH, BR_W, D_MODEL), BR_W ** -0.5 * DN_BETA)
    w_out = nrm((DEPTH, D_MODEL, D_MODEL), D_MODEL ** -0.5 * DN_BETA)
    ln1_g = 1.0 + nrm((DEPTH, D_MODEL), 0.02)
    ln1_b = nrm((DEPTH, D_MODEL), 0.02)
    w_ff_gate = nrm((DEPTH, D_MODEL, D_FF), D_MODEL ** -0.5)
    w_ff_up = nrm((DEPTH, D_MODEL, D_FF), D_MODEL ** -0.5)
    ff_conv_w = nrm((DEPTH, FFN_CONV, D_FF), FFN_CONV ** -0.5)
    ff_conv_b = nrm((DEPTH, D_FF), 0.02)
    w_ff_down = nrm((DEPTH, D_FF, D_MODEL), D_FF ** -0.5 * DN_BETA)
    ln2_g = 1.0 + nrm((DEPTH, D_MODEL), 0.02)
    ln2_b = nrm((DEPTH, D_MODEL), 0.02)
    return {'x_prompt': x_prompt, 'x_sample': x_sample,
            'cache_fox_k': cache_fox_k, 'cache_fox_v': cache_fox_v, 'cache_fox_logf': cache_fox_logf,
            'state_mlstm_c': state_mlstm_c, 'state_mlstm_n': state_mlstm_n, 'state_mlstm_m': state_mlstm_m,
            'state_hgrn_s': state_hgrn_s, 'state_rglru_h': state_rglru_h, 'state_rglru_conv': state_rglru_conv,
            'state_ffn_conv': state_ffn_conv,
            'w_in': w_in, 'b_in': b_in, 'ml_norm_g': ml_norm_g, 'hg_norm_g': hg_norm_g,
            'hg_lb_logits': hg_lb_logits, 'rg_conv_w': rg_conv_w, 'rg_conv_b': rg_conv_b,
            'rg_w_a': rg_w_a, 'rg_b_a': rg_b_a, 'rg_w_x': rg_w_x, 'rg_b_x': rg_b_x, 'rg_lambda': rg_lambda,
            'w_mg': w_mg, 'b_mg': b_mg, 'w_br': w_br, 'w_out': w_out, 'ln1_g': ln1_g, 'ln1_b': ln1_b,
            'w_ff_gate': w_ff_gate, 'w_ff_up': w_ff_up, 'ff_conv_w': ff_conv_w, 'ff_conv_b': ff_conv_b,
            'w_ff_down': w_ff_down, 'ln2_g': ln2_g, 'ln2_b': ln2_b}


def reference(x_prompt, x_sample, cache_fox_k, cache_fox_v, cache_fox_logf, state_mlstm_c, state_mlstm_n,
              state_mlstm_m, state_hgrn_s, state_rglru_h, state_rglru_conv, state_ffn_conv,
              w_in, b_in, ml_norm_g, hg_norm_g, hg_lb_logits, rg_conv_w, rg_conv_b, rg_w_a, rg_b_a,
              rg_w_x, rg_b_x, rg_lambda, w_mg, b_mg, w_br, w_out, ln1_g, ln1_b,
              w_ff_gate, w_ff_up, ff_conv_w, ff_conv_b, w_ff_down, ln2_g, ln2_b):
    lbs = hgrn_lower_bounds(hg_lb_logits)
    bp = x_prompt.shape[0]
    yp, ys = x_prompt, x_sample
    p_new, s_new = [], []
    for l in range(DEPTH):
        weights = (w_in[l], b_in[l], ml_norm_g[l], hg_norm_g[l], rg_conv_w[l], rg_conv_b[l],
                   rg_w_a[l], rg_b_a[l], rg_w_x[l], rg_b_x[l], rg_lambda[l], w_mg[l], b_mg[l],
                   w_br[l], w_out[l], ln1_g[l], ln1_b[l], w_ff_gate[l], w_ff_up[l], ff_conv_w[l],
                   ff_conv_b[l], w_ff_down[l], ln2_g[l], ln2_b[l])
        yp, st_p = trunk_layer(
            yp, None,
            jnp.zeros((bp, ML_HEADS, ML_DK, ML_DV), F32), jnp.zeros((bp, ML_HEADS, ML_DK), F32),
            jnp.zeros((bp, ML_HEADS), F32), jnp.zeros((bp, HG_HEADS, HG_DK, HG_DV), F32),
            jnp.zeros((bp, RG_W), F32), jnp.zeros((bp, RG_CONV - 1, RG_W), x_prompt.dtype),
            jnp.zeros((bp, FFN_CONV - 1, D_FF), x_prompt.dtype), lbs[l], *weights)
        p_new.append(st_p)
        ys, st_s = trunk_layer(
            ys, (cache_fox_k[l], cache_fox_v[l], cache_fox_logf[l]),
            state_mlstm_c[l], state_mlstm_n[l], state_mlstm_m[l], state_hgrn_s[l],
            state_rglru_h[l], state_rglru_conv[l], state_ffn_conv[l], lbs[l], *weights)
        s_new.append(st_s)
    n_st = len(p_new[0])
    (p_fox_k, p_fox_v, p_fox_logf, p_ml_c, p_ml_n, p_ml_m, p_hg_s, p_rg_h, p_rg_conv,
     p_ff_conv) = [jnp.stack([st[j] for st in p_new]) for j in range(n_st)]
    (s_fox_k, s_fox_v, s_fox_logf, s_ml_c, s_ml_n, s_ml_m, s_hg_s, s_rg_h, s_rg_conv,
     s_ff_conv) = [jnp.stack([st[j] for st in s_new]) for j in range(n_st)]
    return (yp, ys,
            p_fox_k, p_fox_v, p_fox_logf, p_ml_c, p_ml_n, p_ml_m, p_hg_s, p_rg_h, p_rg_conv, p_ff_conv,
            s_fox_k, s_fox_v, s_fox_logf, s_ml_c, s_ml_n, s_ml_m, s_hg_s, s_rg_h, s_rg_conv, s_ff_conv)
```

```python
import functools

import numpy as np
import jax
import jax.numpy as jnp
from jax import lax
from jax.experimental import pallas as pl
from jax.experimental.pallas import tpu as pltpu

F32 = jnp.float32
BF16 = jnp.bfloat16

D_MODEL = 1024
BR_W = D_MODEL // 2
ML_HEADS = 4
ML_DV = BR_W // ML_HEADS
ML_DK = ML_DV // 2
ML_CHUNK = 64
FX_HD = 64
FX_HEADS = BR_W // FX_HD
FX_PAST_TILE = 256
HG_HEADS = 4
HG_DK = BR_W // HG_HEADS
HG_BLOCK = 16
HG_CHUNK = 64
RG_W = BR_W
RG_BLOCKS = 8
RG_BD = RG_W // RG_BLOCKS
RG_CONV = 4
RG_C = 8.0
D_FF = 2 * D_MODEL
FFN_CONV = 3
N_BRANCH = 4
PROJ_SIZES = (ML_HEADS * ML_DK, ML_HEADS * ML_DK, BR_W, BR_W, ML_HEADS, ML_HEADS,
              BR_W, BR_W, BR_W, FX_HEADS,
              BR_W, BR_W, BR_W, BR_W,
              RG_W, RG_W)
NEG_BIG = -1e30
TINY = 1e-30

LANE = 128
SMALL_W = LANE
N_SMALL = 2 * ML_HEADS + FX_HEADS
MAIN_SEGS = (0, 1, 2, 3, 6, 10, 11, 12, 13, 14, 15)
SMALL_SEGS = (4, 5, 9)
N_MAIN = sum(PROJ_SIZES[i] for i in MAIN_SEGS)
PA_W = N_MAIN + SMALL_W
COL_BLK = 512
VMEM_LIMIT = 56 * 1024 * 1024


def _cparams(*sem):
    return pltpu.CompilerParams(dimension_semantics=sem, vmem_limit_bytes=VMEM_LIMIT)


def _dot(a, b):
    return jnp.dot(a, b, preferred_element_type=F32)


def _dot_nt(a, b):
    return lax.dot_general(a, b, (((1,), (1,)), ((), ())), preferred_element_type=F32)


def _dot_tn(a, b):
    return lax.dot_general(a, b, (((0,), (0,)), ((), ())), preferred_element_type=F32)


def _dot_hi(a, b):
    return jnp.dot(a, b, precision=lax.Precision.HIGHEST, preferred_element_type=F32)


def _tri(n, lower, block=None):
    r = lax.broadcasted_iota(jnp.int32, (n, n), 0)
    c = lax.broadcasted_iota(jnp.int32, (n, n), 1)
    m = (r >= c) if lower else (r <= c)
    if block is not None:
        m = m & ((r // block) == (c // block))
    return m.astype(F32)


def _dot_hi_nt(a, b):
    return lax.dot_general(a, b, (((1,), (1,)), ((), ())), precision=lax.Precision.HIGHEST,
                           preferred_element_type=F32)


_log_sigmoid = jax.nn.log_sigmoid
_sigmoid = jax.nn.sigmoid
_gelu = jax.nn.gelu


def _layer_norm(v, g, b):
    mu = jnp.mean(v, -1, keepdims=True)
    c = v - mu
    var = jnp.mean(c * c, -1, keepdims=True)
    return c * lax.rsqrt(var + 1e-5) * g + b


def _proj_kernel(x_ref, w_ref, b_ref, wst_ref, bst_ref,
                 pa_ref, k32_ref, v32_ref, kb_ref, vb_ref, st_ref):
    xb = x_ref[0].astype(BF16)
    for c0 in range(0, PA_W, COL_BLK):
        c1 = min(c0 + COL_BLK, PA_W)
        pa_ref[0, :, c0:c1] = _dot(xb, w_ref[:, c0:c1]) + b_ref[:, c0:c1]
    k = _dot(xb, w_ref[:, PA_W:PA_W + BR_W]) + b_ref[:, PA_W:PA_W + BR_W]
    k32_ref[0] = k
    kb_ref[0] = k.astype(BF16)
    v = _dot(xb, w_ref[:, PA_W + BR_W:]) + b_ref[:, PA_W + BR_W:]
    v32_ref[0] = v
    vb_ref[0] = v.astype(BF16)
    st_ref[0] = _dot_nt(wst_ref[...], xb) + bst_ref[...]


def _proj(x3, w, b, wst, bst):
    bsz, t, _ = x3.shape
    tt = min(t, 256)
    nw = w.shape[1]
    seq = lambda width: pl.BlockSpec((1, tt, width), lambda i, j: (i, j, 0))
    const = lambda shape: pl.BlockSpec(shape, lambda i, j: (0, 0))
    sds = lambda width, dt: jax.ShapeDtypeStruct((bsz, t, width), dt)
    return pl.pallas_call(
        _proj_kernel,
        grid=(bsz, t // tt),
        in_specs=[seq(D_MODEL), const((D_MODEL, nw)), const((1, nw)),
                  const((2 * 8, D_MODEL)), const((2 * 8, 1))],
        out_specs=[seq(PA_W), seq(BR_W), seq(BR_W), seq(BR_W), seq(BR_W),
                   pl.BlockSpec((1, 2 * 8, tt), lambda i, j: (i, 0, j))],
        out_shape=[sds(PA_W, F32), sds(BR_W, F32), sds(BR_W, F32), sds(BR_W, BF16), sds(BR_W, BF16),
                   jax.ShapeDtypeStruct((bsz, 2 * 8, t), F32)],
        compiler_params=_cparams("arbitrary", "arbitrary"),
        name="proj",
    )(x3, w, b, wst, bst)


def _mlstm_kernel(q_ref, k_ref, v_ref, o_ref, sm_ref, st_ref, c0_ref, n0_ref, m0_ref, g_ref,
                  y_ref, c_ref, n_ref, m_ref):
    tt = q_ref.shape[1]
    L = ML_CHUNK

    @pl.when(pl.program_id(1) == 0)
    def _():
        c_ref[...] = c0_ref[...]
        n_ref[...] = n0_ref[...]
        m_ref[...] = m0_ref[...]

    tri_l = _tri(L, True)
    tri_u = _tri(L, False)
    causal = tri_l > 0.5
    for c in range(tt // L):
        r0 = c * L
        sm = sm_ref[0, r0:r0 + L, :]
        st = st_ref[0, :, r0:r0 + L]
        i_col, lf_col = sm[:, 0:ML_HEADS], _log_sigmoid(sm[:, ML_HEADS:2 * ML_HEADS])
        i_row, lf_row = st[0:ML_HEADS, :], _log_sigmoid(st[ML_HEADS:2 * ML_HEADS, :])
        fcum_col = _dot_hi(tri_l, lf_col)
        fcum_row = _dot_hi(lf_row, tri_u)
        m_all = m_ref[0]
        m_new = []
        for h in range(ML_HEADS):
            fc, fr = fcum_col[:, h:h + 1], fcum_row[h:h + 1, :]
            ic, ir = i_col[:, h:h + 1], i_row[h:h + 1, :]
            qh = q_ref[0, r0:r0 + L, h * ML_DK:(h + 1) * ML_DK].astype(BF16)
            kh = k_ref[0, r0:r0 + L, h * ML_DK:(h + 1) * ML_DK] * (ML_DK ** -0.5)
            vh = v_ref[0, r0:r0 + L, h * ML_DV:(h + 1) * ML_DV].astype(BF16)
            cst = c_ref[0, h]
            nst = n_ref[0, h:h + 1, :]
            d = jnp.where(causal, fc - fr + ir, NEG_BIG)
            prev = fc + m_all[:, h:h + 1]
            m_t = jnp.maximum(prev, jnp.max(d, -1, keepdims=True))
            w = jnp.exp(d - m_t) * _dot_nt(qh, kh.astype(BF16))
            g = jnp.exp(prev - m_t)
            num = _dot(w.astype(BF16), vh) + g * _dot(qh, cst.astype(BF16))
            qf = q_ref[0, r0:r0 + L, h * ML_DK:(h + 1) * ML_DK]
            den = jnp.sum(w, -1, keepdims=True) + g * jnp.sum(qf * nst, -1, keepdims=True)
            hh = num / jnp.maximum(jnp.abs(den), jnp.exp(-m_t))
            m_last, fc_last, g_last = m_t[L - 1:L, :], fc[L - 1:L, :], g[L - 1:L, :]
            ks = jnp.exp(fc_last - fc + ic - m_last) * kh
            c_ref[0, h] = g_last * cst + _dot_tn(ks.astype(BF16), vh)
            n_ref[0, h:h + 1, :] = g_last * nst + jnp.sum(ks, 0, keepdims=True)
            m_new.append(m_last)
            mu = jnp.mean(hh, -1, keepdims=True)
            hc = hh - mu
            var = jnp.mean(hc * hc, -1, keepdims=True)
            lanes = slice(h * ML_DV, (h + 1) * ML_DV)
            yv = _sigmoid(o_ref[0, r0:r0 + L, lanes]) * (hc * lax.rsqrt(var + 1e-5) * g_ref[:, lanes])
            y_ref[0, r0:r0 + L, lanes] = yv.astype(y_ref.dtype)
        head = lax.broadcasted_iota(jnp.int32, (1, ML_HEADS), 1)
        m_vec = m_new[0]
        for h in range(1, ML_HEADS):
            m_vec = jnp.where(head == h, m_new[h], m_vec)
        m_ref[0] = m_vec


def _mlstm(pa3, st, c0, n0, m0, g):
    b, t, _ = pa3.shape
    tt = min(t, 256)
    nt = t // tt
    seq = lambda width, blk: pl.BlockSpec((1, tt, width), lambda i, j: (i, j, blk))
    st_spec = pl.BlockSpec((1, 2 * 8, tt), lambda i, j: (i, 0, j))
    c_spec = pl.BlockSpec((1, ML_HEADS, ML_DK, ML_DV), lambda i, j: (i, 0, 0, 0))
    n_spec = pl.BlockSpec((1, ML_HEADS, ML_DK), lambda i, j: (i, 0, 0))
    m_spec = pl.BlockSpec((1, 1, ML_HEADS), lambda i, j: (i, 0, 0))
    return pl.pallas_call(
        _mlstm_kernel,
        grid=(b, nt),
        in_specs=[seq(256, 0), seq(256, 1), seq(512, 1), seq(512, 2), seq(SMALL_W, N_MAIN // SMALL_W),
                  st_spec, c_spec, n_spec, m_spec, pl.BlockSpec((1, BR_W), lambda i, j: (0, 0))],
        out_specs=[pl.BlockSpec((1, tt, BR_W), lambda i, j: (i, j, 0)), c_spec, n_spec, m_spec],
        out_shape=[jax.ShapeDtypeStruct((b, t, BR_W), BF16),
                   jax.ShapeDtypeStruct(c0.shape, F32), jax.ShapeDtypeStruct(n0.shape, F32),
                   jax.ShapeDtypeStruct(m0.shape, F32)],
        compiler_params=_cparams("arbitrary", "arbitrary"),
        name="mlstm",
    )(pa3, pa3, pa3, pa3, pa3, st, c0, n0, m0, g)


def _fox_kernel(*refs, n_past):
    if n_past:
        (q_ref, sm_ref, st_ref, kb_ref, vb_ref, kp_ref, vp_ref, lfp_ref,
         y_ref, lf_ref, crow_ref, ccol_ref, crw_ref) = refs
    else:
        (q_ref, sm_ref, st_ref, kb_ref, vb_ref,
         y_ref, lf_ref, crow_ref, ccol_ref, crw_ref) = refs
    tq = q_ref.shape[1]
    tk = tq
    tkp = FX_PAST_TILE
    single = kb_ref.shape[1] == tq
    t_idx = 0 if single else pl.program_id(1)
    tri_l = _tri(tq, True)
    tri_u = _tri(tq, False)

    def init():
        ccol_ref[...] = jnp.zeros_like(ccol_ref)
        crw_ref[...] = jnp.zeros_like(crw_ref)
        if n_past:
            tri_p = _tri(tkp, False)
            for j in range(n_past // tkp):
                lfp = lfp_ref[0, :, j * tkp:(j + 1) * tkp]
                cr_p = crw_ref[...] + _dot_hi(lfp, tri_p)
                crow_ref[:, j * tkp:(j + 1) * tkp] = cr_p
                crw_ref[...] = cr_p[:, tkp - 1:tkp]
                ccol_ref[...] += _dot_hi_nt(jnp.ones((1, tkp), F32), lfp)

    if single:
        init()
    else:
        pl.when(t_idx == 0)(init)

    lf_col = _log_sigmoid(sm_ref[0, :, 2 * ML_HEADS:N_SMALL])
    lf_row = _log_sigmoid(st_ref[0, 2 * ML_HEADS:N_SMALL, :])
    lf_ref[0] = lf_col
    cq = ccol_ref[...] + _dot_hi(tri_l, lf_col)
    cr = crw_ref[...] + _dot_hi(lf_row, tri_u)
    ccol_ref[...] = cq[tq - 1:tq, :]
    crw_ref[...] = cr[:, tq - 1:tq]
    new0 = n_past if single else pl.multiple_of(n_past + t_idx * tq, tq)
    crow_ref[:, pl.ds(new0, tq)] = cr

    lane = lax.broadcasted_iota(jnp.int32, (1, 2 * FX_HD), 1)
    lo = lane < FX_HD
    diag = tri_l > 0.5
    scale = FX_HD ** -0.5
    n_kv_past = n_past // tkp

    for p in range(FX_HEADS // 2):
        lanes = slice(p * 2 * FX_HD, (p + 1) * 2 * FX_HD)
        qp = q_ref[0, :, lanes] * scale
        q2 = (jnp.where(lo, qp, 0.0).astype(BF16), jnp.where(lo, 0.0, qp).astype(BF16))
        cq2 = (cq[:, 2 * p:2 * p + 1], cq[:, 2 * p + 1:2 * p + 2])

        def tile(carry, kt, vt, ck2, masked):
            m0, l0, m1, l1, acc = carry
            ms, ls, ps, al = [], [], [], []
            for e, (m_, l_) in enumerate(((m0, l0), (m1, l1))):
                s = _dot_nt(q2[e], kt) + cq2[e] - ck2[e]
                if masked:
                    s = jnp.where(diag, s, NEG_BIG)
                mn = jnp.maximum(m_, jnp.max(s, -1, keepdims=True))
                pe = jnp.exp(s - mn)
                a = jnp.exp(m_ - mn)
                ms.append(mn)
                ls.append(a * l_ + jnp.sum(pe, -1, keepdims=True))
                ps.append(pe.astype(BF16))
                al.append(a)
            v_lo = jnp.where(lo, vt, jnp.zeros_like(vt))
            v_hi = jnp.where(lo, jnp.zeros_like(vt), vt)
            acc = jnp.where(lo, al[0], al[1]) * acc + _dot(ps[0], v_lo) + _dot(ps[1], v_hi)
            return ms[0], ls[0], ms[1], ls[1], acc

        col = lambda val: jnp.full((tq, 1), val, F32)
        carry = (col(NEG_BIG), col(0.0), col(NEG_BIG), col(0.0), jnp.zeros((tq, 2 * FX_HD), F32))

        if n_past:
            def past_body(j, carry):
                k0 = pl.multiple_of(j * tkp, tkp)
                kt = kp_ref[0, pl.ds(k0, tkp), lanes].astype(BF16)
                vt = vp_ref[0, pl.ds(k0, tkp), lanes].astype(BF16)
                ck = crow_ref[2 * p:2 * p + 2, pl.ds(k0, tkp)]
                return tile(carry, kt, vt, (ck[0:1], ck[1:2]), False)
            carry = lax.fori_loop(0, n_kv_past, past_body, carry)

        if not single:
            def new_body(j, carry):
                k0 = pl.multiple_of(j * tk, tk)
                kt = kb_ref[0, pl.ds(k0, tk), lanes]
                vt = vb_ref[0, pl.ds(k0, tk), lanes]
                ck = crow_ref[2 * p:2 * p + 2, pl.ds(pl.multiple_of(n_past + j * tk, tk), tk)]
                return tile(carry, kt, vt, (ck[0:1], ck[1:2]), False)
            carry = lax.fori_loop(0, t_idx, new_body, carry)

        k0 = 0 if single else pl.multiple_of(t_idx * tk, tk)
        kt = kb_ref[0, pl.ds(k0, tk), lanes]
        vt = vb_ref[0, pl.ds(k0, tk), lanes]
        ck = crow_ref[2 * p:2 * p + 2, pl.ds(new0, tk)]
        m0, l0, m1, l1, acc = tile(carry, kt, vt, (ck[0:1], ck[1:2]), True)
        y_ref[0, :, lanes] = (acc / jnp.where(lo, l0, l1)).astype(y_ref.dtype)


def _fox(pa3, st, kb3, vb3, past=None):
    b, t, _ = pa3.shape
    tq = min(t, 256)
    nt = t // tq
    n_past = 0 if past is None else past[0].shape[1]
    seq = lambda width, blk: pl.BlockSpec((1, tq, width), lambda i, j: (i, j, blk))
    full = lambda n, width: pl.BlockSpec((1, n, width), lambda i, j: (i, 0, 0))
    in_specs = [seq(BR_W, 3), seq(SMALL_W, N_MAIN // SMALL_W),
                pl.BlockSpec((1, 2 * 8, tq), lambda i, j: (i, 0, j)),
                full(t, BR_W), full(t, BR_W)]
    args = [pa3, pa3, st, kb3, vb3]
    assert n_past % FX_PAST_TILE == 0
    if n_past:
        in_specs += [full(n_past, BR_W), full(n_past, BR_W),
                     pl.BlockSpec((1, FX_HEADS, n_past), lambda i, j: (i, 0, 0))]
        args += list(past)
    return pl.pallas_call(
        functools.partial(_fox_kernel, n_past=n_past),
        grid=(b, nt),
        in_specs=in_specs,
        out_specs=[pl.BlockSpec((1, tq, BR_W), lambda i, j: (i, j, 0)),
                   pl.BlockSpec((1, tq, FX_HEADS), lambda i, j: (i, j, 0))],
        out_shape=[jax.ShapeDtypeStruct((b, t, BR_W), BF16),
                   jax.ShapeDtypeStruct((b, t, FX_HEADS), F32)],
        scratch_shapes=[pltpu.VMEM((FX_HEADS, n_past + t), F32),
                        pltpu.VMEM((1, FX_HEADS), F32), pltpu.VMEM((FX_HEADS, 1), F32)],
        compiler_params=_cparams("arbitrary", "arbitrary"),
        name="fox",
    )(*args)


def _hgrn_kernel(f_ref, i_ref, q_ref, g_ref, lb_ref, gn_ref, s0_ref,
                 y_ref, s_ref, st_ref, qe_ref, kd_ref, ke_ref, eb_ref, o_ref):
    tt = f_ref.shape[1]
    L = HG_CHUNK
    nb = L // HG_BLOCK

    @pl.when(pl.program_id(1) == 0)
    def _():
        for h in range(HG_HEADS):
            st_ref[h] = s0_ref[0, h].T

    tri_blk = _tri(L, True, HG_BLOCK)
    last_sel = ((lax.broadcasted_iota(jnp.int32, (L, L), 1) % HG_BLOCK == HG_BLOCK - 1)
                & (lax.broadcasted_iota(jnp.int32, (L, L), 0) // HG_BLOCK
                   == lax.broadcasted_iota(jnp.int32, (L, L), 1) // HG_BLOCK)).astype(F32)
    blk_causal = tri_blk > 0.5
    lb = lb_ref[...]
    for c in range(tt // L):
        r0 = c * L
        z = f_ref[0, r0:r0 + L, :]
        logf = jnp.log(jnp.maximum(lb + (1.0 - lb) * _sigmoid(z), TINY))
        kk = (1.0 - lb) * _sigmoid(-z)
        qv = q_ref[0, r0:r0 + L, :]
        qq = qv * _sigmoid(qv)
        bcum = _dot_hi(tri_blk, logf)
        blast = _dot_hi(last_sel, bcum)
        qe_ref[...] = (qq * jnp.exp(bcum)).astype(BF16)
        kd_ref[...] = (kk * jnp.exp(-bcum)).astype(BF16)
        ke_ref[...] = (kk * jnp.exp(blast - bcum)).astype(BF16)
        eb_ref[...] = jnp.exp(blast)
        for h in range(HG_HEADS):
            lanes = slice(h * HG_DK, (h + 1) * HG_DK)
            vh = i_ref[0, r0:r0 + L, lanes].astype(BF16)
            a = jnp.where(blk_causal, _dot_nt(qe_ref[:, lanes], kd_ref[:, lanes]), 0.0)
            o_ref[:, lanes] = _dot(a.astype(BF16), vh)
            for j in range(nb):
                rows = slice(j * HG_BLOCK, (j + 1) * HG_BLOCK)
                s_t = st_ref[h]
                o_ref[rows, lanes] += _dot_nt(qe_ref[rows, lanes], s_t.astype(BF16))
                eb = eb_ref[j * HG_BLOCK:j * HG_BLOCK + 1, lanes]
                st_ref[h] = eb * s_t + _dot_tn(vh[rows], ke_ref[rows, lanes])
            o = o_ref[:, lanes]
            rms = lax.rsqrt(jnp.mean(o * o, -1, keepdims=True) + 1e-6)
            gv = g_ref[0, r0:r0 + L, lanes]
            y_ref[0, r0:r0 + L, lanes] = (o * rms * gn_ref[:, lanes] * (gv * _sigmoid(gv))).astype(y_ref.dtype)

    @pl.when(pl.program_id(1) == pl.num_programs(1) - 1)
    def _():
        for h in range(HG_HEADS):
            s_ref[0, h] = st_ref[h].T


def _hgrn(pa3, lb, gn, s0):
    b, t, _ = pa3.shape
    tt = min(t, 256)
    nt = t // tt
    seq = lambda blk: pl.BlockSpec((1, tt, BR_W), lambda i, j: (i, j, blk))
    vec = pl.BlockSpec((1, BR_W), lambda i, j: (0, 0))
    s_spec = pl.BlockSpec((1, HG_HEADS, HG_DK, HG_DK), lambda i, j: (i, 0, 0, 0))
    return pl.pallas_call(
        _hgrn_kernel,
        grid=(b, nt),
        in_specs=[seq(4), seq(5), seq(6), seq(7), vec, vec, s_spec],
        out_specs=[pl.BlockSpec((1, tt, BR_W), lambda i, j: (i, j, 0)), s_spec],
        out_shape=[jax.ShapeDtypeStruct((b, t, BR_W), BF16), jax.ShapeDtypeStruct(s0.shape, F32)],
        scratch_shapes=[pltpu.VMEM((HG_HEADS, HG_DK, HG_DK), F32),
                        pltpu.VMEM((HG_CHUNK, BR_W), BF16), pltpu.VMEM((HG_CHUNK, BR_W), BF16),
                        pltpu.VMEM((HG_CHUNK, BR_W), BF16), pltpu.VMEM((HG_CHUNK, BR_W), F32),
                        pltpu.VMEM((HG_CHUNK, BR_W), F32)],
        compiler_params=_cparams("arbitrary", "arbitrary"),
        name="hgrn",
    )(pa3, pa3, pa3, pa3, lb, gn, s0)


RG_PAD = 8


def _rglru_kernel(x_ref, g_ref, cw_ref, cb_ref, wa_ref, ba_ref, wx_ref, bx_ref, lam_ref,
                  h0_ref, buf0_ref, y_ref, h_ref, buf_ref, xp_ref):
    tt = x_ref.shape[1]
    hist = RG_CONV - 1

    @pl.when(pl.program_id(1) == 0)
    def _():
        h_ref[...] = h0_ref[...]
        xp_ref[RG_PAD - hist:RG_PAD, :] = buf0_ref[0]

    x = x_ref[0]
    xp_ref[RG_PAD:RG_PAD + tt, :] = x
    u = cb_ref[...] + x * cw_ref[hist:hist + 1, :]
    for j in range(hist):
        u = u + xp_ref[RG_PAD - hist + j:RG_PAD - hist + j + tt, :] * cw_ref[j:j + 1, :]
    new_buf = xp_ref[RG_PAD + tt - hist:RG_PAD + tt, :]
    xp_ref[RG_PAD - hist:RG_PAD, :] = new_buf
    buf_ref[0] = new_buf

    ub = u.astype(BF16)
    r = _sigmoid(_dot(ub, wa_ref[...]) + ba_ref[...])
    ig = _sigmoid(_dot(ub, wx_ref[...]) + bx_ref[...])
    log_a = -RG_C * r * jax.nn.softplus(-lam_ref[...])
    a = jnp.exp(log_a)
    bt = jnp.sqrt(jnp.maximum(-jnp.tanh(log_a) * (a * a + 1.0), 0.0)) * (ig * u)
    row = lax.broadcasted_iota(jnp.int32, (tt, 1), 0)
    s = 1
    while s < tt:
        keep = row >= s
        a_sh = jnp.where(keep, pltpu.roll(a, s, 0), 1.0)
        b_sh = jnp.where(keep, pltpu.roll(bt, s, 0), 0.0)
        bt = a * b_sh + bt
        a = a * a_sh
        s *= 2
    h = a * h_ref[0] + bt
    h_ref[0] = h[tt - 1:tt, :]
    y_ref[0] = (h * _gelu(g_ref[0])).astype(y_ref.dtype)


def _rglru(pa3, cw, cb, wa, ba, wx, bx, lam, h0, buf0):
    b, t, _ = pa3.shape
    tt = min(t, 256)
    nt = t // tt
    seq = lambda blk: pl.BlockSpec((1, tt, RG_W), lambda i, j: (i, j, blk))
    vec = pl.BlockSpec((1, RG_W), lambda i, j: (0, 0))
    mat = pl.BlockSpec((RG_W, RG_W), lambda i, j: (0, 0))
    h_spec = pl.BlockSpec((1, 1, RG_W), lambda i, j: (i, 0, 0))
    buf_spec = pl.BlockSpec((1, RG_CONV - 1, RG_W), lambda i, j: (i, 0, 0))
    return pl.pallas_call(
        _rglru_kernel,
        grid=(b, nt),
        in_specs=[seq(8), seq(9), pl.BlockSpec((RG_CONV, RG_W), lambda i, j: (0, 0)), vec,
                  mat, vec, mat, vec, vec, h_spec, buf_spec],
        out_specs=[pl.BlockSpec((1, tt, RG_W), lambda i, j: (i, j, 0)), h_spec, buf_spec],
        out_shape=[jax.ShapeDtypeStruct((b, t, RG_W), BF16),
                   jax.ShapeDtypeStruct(h0.shape, F32), jax.ShapeDtypeStruct(buf0.shape, F32)],
        scratch_shapes=[pltpu.VMEM((RG_PAD + tt, RG_W), F32)],
        compiler_params=_cparams("arbitrary", "arbitrary"),
        name="rglru",
    )(pa3, pa3, cw, cb, wa, ba, wx, bx, lam, h0, buf0)


def _merge_kernel(x_ref, y0_ref, y1_ref, y2_ref, y3_ref, wmg_ref, bmg_ref, wbr_ref, wout_ref,
                  g_ref, b_ref, o_ref, *, alpha):
    x = x_ref[...]
    xb = x.astype(BF16)
    mix = None
    for m, y_ref in enumerate((y0_ref, y1_ref, y2_ref, y3_ref)):
        gate = _sigmoid(_dot(xb, wmg_ref[m]) + bmg_ref[m])
        term = gate * _dot(y_ref[...], wbr_ref[m])
        mix = term if mix is None else mix + term
    o_ref[...] = _layer_norm(alpha * x + _dot(mix.astype(BF16), wout_ref[...]), g_ref[...], b_ref[...])


def _merge(x2d, ys, wmg, bmg, wbr, wout, g, bb, alpha):
    m = x2d.shape[0]
    tm = min(m, 256)
    row = lambda width: pl.BlockSpec((tm, width), lambda i: (i, 0))
    c2 = lambda shape: pl.BlockSpec(shape, lambda i: (0, 0))
    c3 = lambda shape: pl.BlockSpec(shape, lambda i: (0, 0, 0))
    return pl.pallas_call(
        functools.partial(_merge_kernel, alpha=alpha),
        grid=(m // tm,),
        in_specs=[row(D_MODEL)] + [row(BR_W)] * N_BRANCH +
                 [c3((N_BRANCH, D_MODEL, D_MODEL)), c3((N_BRANCH, 1, D_MODEL)),
                  c3((N_BRANCH, BR_W, D_MODEL)), c2((D_MODEL, D_MODEL)),
                  c2((1, D_MODEL)), c2((1, D_MODEL))],
        out_specs=row(D_MODEL),
        out_shape=jax.ShapeDtypeStruct((m, D_MODEL), F32),
        compiler_params=_cparams("arbitrary"),
        name="merge",
    )(x2d, *ys, wmg, bmg, wbr, wout, g, bb)


FF_PAD = 8


def _ffn_kernel(x_ref, wg_ref, wu_ref, cw_ref, cb_ref, wd_ref, g_ref, b_ref, buf0_ref,
                o_ref, buf_ref, gp_ref, *, alpha):
    tt = x_ref.shape[1]
    hist = FFN_CONV - 1

    @pl.when(pl.program_id(1) == 0)
    def _():
        gp_ref[FF_PAD - hist:FF_PAD, :] = buf0_ref[0]

    x = x_ref[0]
    xb = x.astype(BF16)
    acc = None
    for c0 in range(0, D_FF, COL_BLK):
        cols = slice(c0, c0 + COL_BLK)
        gp = _dot(xb, wg_ref[:, cols])
        gp_ref[FF_PAD:FF_PAD + tt, cols] = gp
        gc = cb_ref[:, cols] + gp * cw_ref[hist:hist + 1, cols]
        for j in range(hist):
            gc = gc + gp_ref[FF_PAD - hist + j:FF_PAD - hist + j + tt, cols] * cw_ref[j:j + 1, cols]
        hmid = (_gelu(gc) * _dot(xb, wu_ref[:, cols])).astype(BF16)
        part = _dot(hmid, wd_ref[cols, :])
        acc = part if acc is None else acc + part
    new_buf = gp_ref[FF_PAD + tt - hist:FF_PAD + tt, :]
    gp_ref[FF_PAD - hist:FF_PAD, :] = new_buf
    buf_ref[0] = new_buf
    o_ref[0] = _layer_norm(alpha * x + acc, g_ref[...], b_ref[...])


def _ffn(x3, wg, wu, cw, cb, wd, g, bb, buf0, alpha):
    b, t, _ = x3.shape
    tt = min(t, 256)
    nt = t // tt
    c2 = lambda shape: pl.BlockSpec(shape, lambda i, j: (0, 0))
    buf_spec = pl.BlockSpec((1, FFN_CONV - 1, D_FF), lambda i, j: (i, 0, 0))
    seq = pl.BlockSpec((1, tt, D_MODEL), lambda i, j: (i, j, 0))
    return pl.pallas_call(
        functools.partial(_ffn_kernel, alpha=alpha),
        grid=(b, nt),
        in_specs=[seq, c2((D_MODEL, D_FF)), c2((D_MODEL, D_FF)), c2((FFN_CONV, D_FF)), c2((1, D_FF)),
                  c2((D_FF, D_MODEL)), c2((1, D_MODEL)), c2((1, D_MODEL)), buf_spec],
        out_specs=[seq, buf_spec],
        out_shape=[jax.ShapeDtypeStruct(x3.shape, F32), jax.ShapeDtypeStruct(buf0.shape, F32)],
        scratch_shapes=[pltpu.VMEM((FF_PAD + tt, D_FF), F32)],
        compiler_params=_cparams("arbitrary", "arbitrary"),
        name="ffn",
    )(x3, wg, wu, cw, cb, wd, g, bb, buf0)


def _prep_layer(l, w_in, b_in, ml_norm_g, hg_norm_g, lbs, rg_conv_w, rg_conv_b, rg_w_a, rg_b_a,
                rg_w_x, rg_b_x, rg_lambda, w_mg, b_mg, w_br, w_out, ln1_g, ln1_b,
                w_ff_gate, w_ff_up, ff_conv_w, ff_conv_b, w_ff_down, ln2_g, ln2_b):
    offs = np.concatenate([[0], np.cumsum(PROJ_SIZES)])
    cols = lambda segs: np.concatenate([np.arange(offs[s], offs[s + 1]) for s in segs])
    main, small, kv = cols(MAIN_SEGS), cols(SMALL_SEGS), cols((7, 8))
    wl, bl = w_in[l], b_in[l]
    zpad = jnp.zeros((D_MODEL, SMALL_W - N_SMALL), F32)
    w = jnp.concatenate([wl[:, main], wl[:, small], zpad, wl[:, kv]], axis=1).astype(BF16)
    bvec = jnp.concatenate([bl[main], bl[small], jnp.zeros((SMALL_W - N_SMALL,), F32), bl[kv]])[None, :]
    wst = wl[:, small].T.astype(BF16)
    bst = bl[small][:, None]

    def block_diag(wb):
        eye = jnp.eye(RG_BLOCKS, dtype=F32)
        return jnp.einsum('nde,nm->ndme', wb, eye).reshape(RG_W, RG_W).astype(BF16)

    row = lambda v: v[None, :].astype(F32)
    return dict(
        w=w, b=bvec, wst=wst, bst=bst, ml_g=row(ml_norm_g[l]), hg_g=row(hg_norm_g[l]), lb=row(lbs[l]),
        rg_cw=rg_conv_w[l], rg_cb=row(rg_conv_b[l]), rg_wa=block_diag(rg_w_a[l]), rg_ba=row(rg_b_a[l]),
        rg_wx=block_diag(rg_w_x[l]), rg_bx=row(rg_b_x[l]), rg_lam=row(rg_lambda[l]),
        wmg=w_mg[l].astype(BF16), bmg=b_mg[l][:, None, :], wbr=w_br[l].astype(BF16),
        wout=w_out[l].astype(BF16), ln1_g=row(ln1_g[l]), ln1_b=row(ln1_b[l]),
        wg=w_ff_gate[l].astype(BF16), wu=w_ff_up[l].astype(BF16), ff_cw=ff_conv_w[l],
        ff_cb=row(ff_conv_b[l]), wd=w_ff_down[l].astype(BF16), ln2_g=row(ln2_g[l]), ln2_b=row(ln2_b[l]))


def _trunk_layer(x, p, fox_past, ml_c, ml_n, ml_m, hg_s, rg_h, rg_buf, ff_buf, alpha):
    b, t, _ = x.shape
    assert t % ML_CHUNK == 0 and (t <= 256 or t % 256 == 0)
    x2d = x.reshape(b * t, D_MODEL)
    pa3, k32, v32, kb, vb, st = _proj(x, p['w'], p['b'], p['wst'], p['bst'])
    y_ml, ml_c, ml_n, ml_m = _mlstm(pa3, st, ml_c, ml_n, ml_m.reshape(b, 1, ML_HEADS), p['ml_g'])
    past = None
    if fox_past is not None:
        k_past, v_past, logf_past = fox_past
        n_past = k_past.shape[1]
        past = (k_past.reshape(b, n_past, BR_W), v_past.reshape(b, n_past, BR_W),
                jnp.swapaxes(logf_past, 1, 2))
    y_fx, f_log = _fox(pa3, st, kb, vb, past)
    y_hg, hg_s = _hgrn(pa3, p['lb'], p['hg_g'], hg_s)
    y_rg, rg_h, rg_buf = _rglru(pa3, p['rg_cw'], p['rg_cb'], p['rg_wa'], p['rg_ba'], p['rg_wx'], p['rg_bx'],
                                p['rg_lam'], rg_h.reshape(b, 1, RG_W), rg_buf)
    ys = [y.reshape(b * t, BR_W) for y in (y_ml, y_fx, y_hg, y_rg)]
    x1 = _merge(x2d, ys, p['wmg'], p['bmg'], p['wbr'], p['wout'], p['ln1_g'], p['ln1_b'], alpha)
    x2, ff_buf = _ffn(x1.reshape(b, t, D_MODEL), p['wg'], p['wu'], p['ff_cw'], p['ff_cb'], p['wd'],
                      p['ln2_g'], p['ln2_b'], ff_buf, alpha)
    state = (k32.reshape(b, t, FX_HEADS, FX_HD), v32.reshape(b, t, FX_HEADS, FX_HD), f_log,
             ml_c, ml_n, ml_m.reshape(b, ML_HEADS), hg_s, rg_h.reshape(b, RG_W), rg_buf, ff_buf)
    return x2, state


def kernel(x_prompt, x_sample, cache_fox_k, cache_fox_v, cache_fox_logf, state_mlstm_c, state_mlstm_n,
           state_mlstm_m, state_hgrn_s, state_rglru_h, state_rglru_conv, state_ffn_conv,
           w_in, b_in, ml_norm_g, hg_norm_g, hg_lb_logits, rg_conv_w, rg_conv_b, rg_w_a, rg_b_a,
           rg_w_x, rg_b_x, rg_lambda, w_mg, b_mg, w_br, w_out, ln1_g, ln1_b,
           w_ff_gate, w_ff_up, ff_conv_w, ff_conv_b, w_ff_down, ln2_g, ln2_b):
    depth = w_in.shape[0]
    alpha = (2 * depth) ** 0.25
    pl_soft = jax.nn.softmax(hg_lb_logits.astype(F32), axis=0)
    lbs = jnp.cumsum(pl_soft, axis=0) - pl_soft[0]
    bp = x_prompt.shape[0]
    yp, ys = x_prompt, x_sample
    p_new, s_new = [], []
    for l in range(depth):
        p = _prep_layer(l, w_in, b_in, ml_norm_g, hg_norm_g, lbs, rg_conv_w, rg_conv_b, rg_w_a, rg_b_a,
                        rg_w_x, rg_b_x, rg_lambda, w_mg, b_mg, w_br, w_out, ln1_g, ln1_b,
                        w_ff_gate, w_ff_up, ff_conv_w, ff_conv_b, w_ff_down, ln2_g, ln2_b)
        z = lambda *shape: jnp.zeros((bp,) + shape, F32)
        yp, st_p = _trunk_layer(yp, p, None, z(ML_HEADS, ML_DK, ML_DV), z(ML_HEADS, ML_DK), z(ML_HEADS),
                                z(HG_HEADS, HG_DK, HG_DK), z(RG_W), z(RG_CONV - 1, RG_W),
                                z(FFN_CONV - 1, D_FF), alpha)
        p_new.append(st_p)
        ys, st_s = _trunk_layer(ys, p, (cache_fox_k[l], cache_fox_v[l], cache_fox_logf[l]),
                                state_mlstm_c[l], state_mlstm_n[l], state_mlstm_m[l], state_hgrn_s[l],
                                state_rglru_h[l], state_rglru_conv[l], state_ffn_conv[l], alpha)
        s_new.append(st_s)
    n_st = len(p_new[0])
    p_out = [jnp.stack([st[j] for st in p_new]) for j in range(n_st)]
    s_out = [jnp.stack([st[j] for st in s_new]) for j in range(n_st)]
    return (yp, ys, *p_out, *s_out)
```

```python
import functools

import numpy as np
import jax
import jax.numpy as jnp
from jax import lax
from jax.experimental import pallas as pl
from jax.experimental.pallas import tpu as pltpu

F32 = jnp.float32
BF16 = jnp.bfloat16

D_MODEL = 1024
BR_W = D_MODEL // 2
ML_HEADS = 4
ML_DV = BR_W // ML_HEADS
ML_DK = ML_DV // 2
ML_CHUNK = 128
FX_HD = 64
FX_HEADS = BR_W // FX_HD
FX_PAST_TILE = 128
FX_KEY_TILE = 128
LOG2E = 1.4426950408889634
HG_HEADS = 4
HG_DK = BR_W // HG_HEADS
HG_BLOCK = 16
HG_CHUNK = 64
RG_W = BR_W
RG_BLOCKS = 8
RG_BD = RG_W // RG_BLOCKS
RG_CONV = 4
RG_C = 8.0
D_FF = 2 * D_MODEL
FFN_CONV = 3
N_BRANCH = 4
PROJ_SIZES = (ML_HEADS * ML_DK, ML_HEADS * ML_DK, BR_W, BR_W, ML_HEADS, ML_HEADS,
              BR_W, BR_W, BR_W, FX_HEADS,
              BR_W, BR_W, BR_W, BR_W,
              RG_W, RG_W)
NEG_BIG = -1e30
TINY = 1e-30

LANE = 128
SMALL_W = LANE
N_SMALL = 2 * ML_HEADS + FX_HEADS
MAIN_SEGS = (0, 1, 2, 3, 6, 10, 11, 12, 13, 14, 15)
SMALL_SEGS = (4, 5, 9)
N_MAIN = sum(PROJ_SIZES[i] for i in MAIN_SEGS)
PA_W = N_MAIN + SMALL_W
COL_BLK = 512
VMEM_LIMIT = 56 * 1024 * 1024


def _cparams(*sem):
    return pltpu.CompilerParams(dimension_semantics=sem, vmem_limit_bytes=VMEM_LIMIT)


def _dot(a, b):
    return jnp.dot(a, b, preferred_element_type=F32)


def _dot_nt(a, b):
    return lax.dot_general(a, b, (((1,), (1,)), ((), ())), preferred_element_type=F32)


def _dot_tn(a, b):
    return lax.dot_general(a, b, (((0,), (0,)), ((), ())), preferred_element_type=F32)


def _dot_hi(a, b):
    return jnp.dot(a, b, precision=lax.Precision.HIGHEST, preferred_element_type=F32)


def _tri(n, lower, block=None):
    r = lax.broadcasted_iota(jnp.int32, (n, n), 0)
    c = lax.broadcasted_iota(jnp.int32, (n, n), 1)
    m = (r >= c) if lower else (r <= c)
    if block is not None:
        m = m & ((r // block) == (c // block))
    return m.astype(F32)


def _dot_hi_nt(a, b):
    return lax.dot_general(a, b, (((1,), (1,)), ((), ())), precision=lax.Precision.HIGHEST,
                           preferred_element_type=F32)


_log_sigmoid = jax.nn.log_sigmoid
_sigmoid = jax.nn.sigmoid
_gelu = jax.nn.gelu


def _layer_norm(v, g, b):
    mu = jnp.mean(v, -1, keepdims=True)
    c = v - mu
    var = jnp.mean(c * c, -1, keepdims=True)
    return c * lax.rsqrt(var + 1e-5) * g + b


def _proj_kernel(x_ref, w_ref, b_ref, wst_ref, bst_ref,
                 pa_ref, k32_ref, v32_ref, kb_ref, vb_ref, st_ref):
    xb = x_ref[0].astype(BF16)
    for c0 in range(0, PA_W, COL_BLK):
        c1 = min(c0 + COL_BLK, PA_W)
        pa_ref[0, :, c0:c1] = _dot(xb, w_ref[:, c0:c1]) + b_ref[:, c0:c1]
    k = _dot(xb, w_ref[:, PA_W:PA_W + BR_W]) + b_ref[:, PA_W:PA_W + BR_W]
    k32_ref[0] = k
    kb_ref[0] = k.astype(BF16)
    v = _dot(xb, w_ref[:, PA_W + BR_W:]) + b_ref[:, PA_W + BR_W:]
    v32_ref[0] = v
    vb_ref[0] = v.T.astype(BF16)
    st_ref[0] = _dot_nt(wst_ref[...], xb) + bst_ref[...]


def _proj(x3, w, b, wst, bst):
    bsz, t, _ = x3.shape
    tt = min(t, 256)
    nw = w.shape[1]
    seq = lambda width: pl.BlockSpec((1, tt, width), lambda i, j: (i, j, 0))
    const = lambda shape: pl.BlockSpec(shape, lambda i, j: (0, 0))
    sds = lambda width, dt: jax.ShapeDtypeStruct((bsz, t, width), dt)
    return pl.pallas_call(
        _proj_kernel,
        grid=(bsz, t // tt),
        in_specs=[seq(D_MODEL), const((D_MODEL, nw)), const((1, nw)),
                  const((2 * 8, D_MODEL)), const((2 * 8, 1))],
        out_specs=[seq(PA_W), seq(BR_W), seq(BR_W), seq(BR_W),
                   pl.BlockSpec((1, BR_W, tt), lambda i, j: (i, 0, j)),
                   pl.BlockSpec((1, 2 * 8, tt), lambda i, j: (i, 0, j))],
        out_shape=[sds(PA_W, F32), sds(BR_W, F32), sds(BR_W, F32), sds(BR_W, BF16),
                   jax.ShapeDtypeStruct((bsz, BR_W, t), BF16),
                   jax.ShapeDtypeStruct((bsz, 2 * 8, t), F32)],
        compiler_params=_cparams("arbitrary", "arbitrary"),
        name="proj",
    )(x3, w, b, wst, bst)


def _mlstm_kernel(q_ref, k_ref, v_ref, o_ref, sm_ref, st_ref, c0_ref, n0_ref, m0_ref, g_ref,
                  y_ref, c_ref, n_ref, m_ref, ct_ref):
    tt = q_ref.shape[1]
    L = min(tt, ML_CHUNK)
    H = range(ML_HEADS)

    @pl.when(pl.program_id(1) == 0)
    def _():
        for h in H:
            ct_ref[h] = c0_ref[0, h].T
        n_ref[...] = n0_ref[...]
        m_ref[...] = m0_ref[...]

    tri_l = _tri(L, True)
    tri_u = _tri(L, False)
    causal_t = tri_u > 0.5
    for c in range(tt // L):
        r0 = c * L
        sm = sm_ref[0, r0:r0 + L, :]
        st = st_ref[0, :, r0:r0 + L]
        i_col, lf_col = sm[:, 0:ML_HEADS], _log_sigmoid(sm[:, ML_HEADS:2 * ML_HEADS])
        lf_row = _log_sigmoid(st[ML_HEADS:2 * ML_HEADS, :])
        a_cols = i_col - _dot_hi(tri_l, lf_col)
        fcum_row = _dot_hi(lf_row, tri_u)
        m_all = m_ref[0]
        dk = [slice(h * ML_DK, (h + 1) * ML_DK) for h in H]
        dv = [slice(h * ML_DV, (h + 1) * ML_DV) for h in H]
        qf = [q_ref[0, r0:r0 + L, dk[h]] for h in H]
        qb = [x.astype(BF16) for x in qf]
        kf = [k_ref[0, r0:r0 + L, dk[h]] * (ML_DK ** -0.5) for h in H]
        vb = [v_ref[0, r0:r0 + L, dv[h]].astype(BF16) for h in H]
        ct = [ct_ref[h] for h in H]
        nst = [n_ref[0, h:h + 1, :] for h in H]
        s_t = [_dot_nt(kf[h].astype(BF16), qb[h]) for h in H]
        qc_t = [_dot_nt(ct[h].astype(BF16), qb[h]) for h in H]
        qn = [_dot_hi_nt(nst[h], qf[h]) for h in H]
        w_t, g, m_t = [], [], []
        for h in H:
            fr = fcum_row[h:h + 1, :]
            d_t = jnp.where(causal_t, fr + a_cols[:, h:h + 1], NEG_BIG)
            prev = fr + m_all[:, h:h + 1]
            m_t.append(jnp.maximum(prev, jnp.max(d_t, 0, keepdims=True)))
            w_t.append(jnp.exp(d_t - m_t[h]) * s_t[h])
            g.append(jnp.exp(prev - m_t[h]))
        num_t = [_dot_tn(vb[h], w_t[h].astype(BF16)) + g[h] * qc_t[h] for h in H]
        ks = []
        for h in H:
            m_last, fr_last = m_t[h][:, L - 1:L], fcum_row[h:h + 1, L - 1:L]
            ks.append(jnp.exp(fr_last + a_cols[:, h:h + 1] - m_last) * kf[h])
        dct = [_dot_tn(vb[h], ks[h].astype(BF16)) for h in H]
        for h in H:
            g_last = g[h][:, L - 1:L]
            ct_ref[h] = g_last * ct[h] + dct[h]
            n_ref[0, h:h + 1, :] = g_last * nst[h] + jnp.sum(ks[h], 0, keepdims=True)
        for h in H:
            den = jnp.sum(w_t[h], 0, keepdims=True) + g[h] * qn[h]
            h_t = num_t[h] / jnp.maximum(jnp.abs(den), jnp.exp(-m_t[h]))
            mu = jnp.mean(h_t, 0, keepdims=True)
            hc = h_t - mu
            var = jnp.mean(hc * hc, 0, keepdims=True)
            hn = (hc * lax.rsqrt(var + 1e-5)).T
            yv = _sigmoid(o_ref[0, r0:r0 + L, dv[h]]) * (hn * g_ref[:, dv[h]])
            y_ref[0, r0:r0 + L, dv[h]] = yv.astype(y_ref.dtype)
        head = lax.broadcasted_iota(jnp.int32, (1, ML_HEADS), 1)
        m_vec = m_t[0][:, L - 1:L]
        for h in range(1, ML_HEADS):
            m_vec = jnp.where(head == h, m_t[h][:, L - 1:L], m_vec)
        m_ref[0] = m_vec

    @pl.when(pl.program_id(1) == pl.num_programs(1) - 1)
    def _():
        for h in H:
            c_ref[0, h] = ct_ref[h].T


def _mlstm(pa3, st, c0, n0, m0, g):
    b, t, _ = pa3.shape
    tt = min(t, 256)
    nt = t // tt
    seq = lambda width, blk: pl.BlockSpec((1, tt, width), lambda i, j: (i, j, blk))
    st_spec = pl.BlockSpec((1, 2 * 8, tt), lambda i, j: (i, 0, j))
    c_spec = pl.BlockSpec((1, ML_HEADS, ML_DK, ML_DV), lambda i, j: (i, 0, 0, 0))
    n_spec = pl.BlockSpec((1, ML_HEADS, ML_DK), lambda i, j: (i, 0, 0))
    m_spec = pl.BlockSpec((1, 1, ML_HEADS), lambda i, j: (i, 0, 0))
    return pl.pallas_call(
        _mlstm_kernel,
        grid=(b, nt),
        in_specs=[seq(256, 0), seq(256, 1), seq(512, 1), seq(512, 2), seq(SMALL_W, N_MAIN // SMALL_W),
                  st_spec, c_spec, n_spec, m_spec, pl.BlockSpec((1, BR_W), lambda i, j: (0, 0))],
        out_specs=[pl.BlockSpec((1, tt, BR_W), lambda i, j: (i, j, 0)), c_spec, n_spec, m_spec],
        out_shape=[jax.ShapeDtypeStruct((b, t, BR_W), BF16),
                   jax.ShapeDtypeStruct(c0.shape, F32), jax.ShapeDtypeStruct(n0.shape, F32),
                   jax.ShapeDtypeStruct(m0.shape, F32)],
        scratch_shapes=[pltpu.VMEM((ML_HEADS, ML_DV, ML_DK), F32)],
        compiler_params=_cparams("arbitrary", "arbitrary"),
        name="mlstm",
    )(pa3, pa3, pa3, pa3, pa3, st, c0, n0, m0, g)


def _fox_kernel(*refs, n_past, tk):
    if n_past:
        (q_ref, sm_ref, st_ref, kb_ref, vt_ref, kp_ref, vp_ref, lfp_ref,
         y_ref, lf_ref, ccol_ref, crun_ref, rrun_ref, yt_ref, qm_ref, m_ref, l_ref) = refs
    else:
        (q_ref, sm_ref, st_ref, kb_ref, vt_ref,
         y_ref, lf_ref, ccol_ref, crun_ref, rrun_ref, yt_ref, qm_ref, m_ref, l_ref) = refs
    tq = q_ref.shape[1]
    tkp = FX_PAST_TILE
    single = kb_ref.shape[1] == tq
    t_idx = 0 if single else pl.program_id(1)
    tri_l = _tri(tq, True)
    tri_u = _tri(tq, False)

    def init():
        crun_ref[...] = jnp.zeros_like(crun_ref)
        if n_past:
            tri_p = _tri(tkp, True)
            for j in range(n_past // tkp):
                cc_p = crun_ref[...] + _dot_hi(tri_p, lfp_ref[0, j * tkp:(j + 1) * tkp, :])
                ccol_ref[j * tkp:(j + 1) * tkp, :] = cc_p * LOG2E
                crun_ref[...] = cc_p[tkp - 1:tkp, :]
        eye = (lax.broadcasted_iota(jnp.int32, (FX_HEADS, FX_HEADS), 0)
               == lax.broadcasted_iota(jnp.int32, (FX_HEADS, FX_HEADS), 1))
        rrun_ref[...] = jnp.sum(jnp.where(eye, crun_ref[...], 0.0), -1, keepdims=True)

    if single:
        init()
    else:
        pl.when(t_idx == 0)(init)

    lf_col = _log_sigmoid(sm_ref[0, :, 2 * ML_HEADS:N_SMALL])
    lf_row = _log_sigmoid(st_ref[0, 2 * ML_HEADS:N_SMALL, :])
    lf_ref[0] = lf_col
    cc = crun_ref[...] + _dot_hi(tri_l, lf_col)
    cr = rrun_ref[...] + _dot_hi(lf_row, tri_u)
    crun_ref[...] = cc[tq - 1:tq, :]
    rrun_ref[...] = cr[:, tq - 1:tq]
    new0 = n_past if single else pl.multiple_of(n_past + t_idx * tq, tq)
    ccol_ref[pl.ds(new0, tq), :] = cc * LOG2E
    cq_rows = cr * LOG2E

    n_sub = tq // tk
    krow = lax.broadcasted_iota(jnp.int32, (tk, tq), 0)
    qcol = lax.broadcasted_iota(jnp.int32, (tk, tq), 1)
    first_head = lax.broadcasted_iota(jnp.int32, (2 * FX_HD, 1), 0) < FX_HD
    qt = (q_ref[0] * (FX_HD ** -0.5 * LOG2E)).T

    for h in range(FX_HEADS):
        qp = qt[(h // 2) * 2 * FX_HD:(h // 2 + 1) * 2 * FX_HD]
        qm_ref[h] = (jnp.where(first_head, qp, 0.0) if h % 2 == 0 else jnp.where(first_head, 0.0, qp)).astype(BF16)
    yt_ref[...] = jnp.zeros_like(yt_ref)
    m_ref[...] = jnp.full(m_ref.shape, NEG_BIG, F32)
    l_ref[...] = jnp.zeros_like(l_ref)

    def tile(k_pairs, vt_pairs, ck, mask):
        ss = [_dot(k_pairs[h // 2], qm_ref[h]) for h in range(FX_HEADS)]
        ps, alphas = [], []
        for h in range(FX_HEADS):
            m_, l_ = m_ref[h:h + 1, :], l_ref[h:h + 1, :]
            s = ss[h] + (cq_rows[h:h + 1, :] - ck[:, h:h + 1])
            if mask is not None:
                s = jnp.where(mask, s, NEG_BIG)
            mn = jnp.maximum(m_, jnp.max(s, 0, keepdims=True))
            pe = jnp.exp2(s - mn)
            a = jnp.exp2(m_ - mn)
            m_ref[h:h + 1, :] = mn
            l_ref[h:h + 1, :] = a * l_ + jnp.sum(pe, 0, keepdims=True)
            ps.append(pe.astype(BF16))
            alphas.append(a)
        for h in range(FX_HEADS):
            p, e = divmod(h, 2)
            hrows = slice(h * FX_HD, (h + 1) * FX_HD)
            yt_ref[hrows, :] = alphas[h] * yt_ref[hrows, :] + _dot(vt_pairs[p][e * FX_HD:(e + 1) * FX_HD, :], ps[h])

    pair_lanes = [slice(p * 2 * FX_HD, (p + 1) * 2 * FX_HD) for p in range(FX_HEADS // 2)]
    if n_past:
        def past_body(j, _):
            k0 = pl.multiple_of(j * tkp, tkp)
            ks = [kp_ref[0, pl.ds(k0, tkp), pp].astype(BF16) for pp in pair_lanes]
            vs = [vp_ref[0, pl.ds(k0, tkp), pp].T.astype(BF16) for pp in pair_lanes]
            tile(ks, vs, ccol_ref[pl.ds(k0, tkp), :], None)
            return 0
        lax.fori_loop(0, n_past // tkp, past_body, 0)

    if not single:
        def new_body(j, _):
            k0 = pl.multiple_of(j * tk, tk)
            ks = [kb_ref[0, pl.ds(k0, tk), pp] for pp in pair_lanes]
            vs = [vt_ref[0, pp, pl.ds(k0, tk)] for pp in pair_lanes]
            tile(ks, vs, ccol_ref[pl.ds(pl.multiple_of(n_past + j * tk, tk), tk), :], None)
            return 0
        lax.fori_loop(0, t_idx * n_sub, new_body, 0)

    for i in range(n_sub):
        k0 = i * tk if single else pl.multiple_of(t_idx * tq + i * tk, tk)
        ks = [kb_ref[0, pl.ds(k0, tk), pp] for pp in pair_lanes]
        vs = [vt_ref[0, pp, pl.ds(k0, tk)] for pp in pair_lanes]
        tile(ks, vs, ccol_ref[pl.ds(n_past + k0, tk), :], krow + i * tk <= qcol)
    for h in range(FX_HEADS):
        hrows = slice(h * FX_HD, (h + 1) * FX_HD)
        yt_ref[hrows, :] = yt_ref[hrows, :] / l_ref[h:h + 1, :]
    y_ref[0] = yt_ref[...].T.astype(y_ref.dtype)


def _fox(pa3, st, kb3, vt3, past=None):
    b, t, _ = pa3.shape
    tq = min(t, 256)
    tk = min(tq, FX_KEY_TILE)
    nt = t // tq
    n_past = 0 if past is None else past[0].shape[1]
    seq = lambda width, blk: pl.BlockSpec((1, tq, width), lambda i, j: (i, j, blk))
    full = lambda n, width: pl.BlockSpec((1, n, width), lambda i, j: (i, 0, 0))
    in_specs = [seq(BR_W, 3), seq(SMALL_W, N_MAIN // SMALL_W),
                pl.BlockSpec((1, 2 * 8, tq), lambda i, j: (i, 0, j)),
                full(t, BR_W), full(BR_W, t)]
    args = [pa3, pa3, st, kb3, vt3]
    assert n_past % FX_PAST_TILE == 0
    if n_past:
        in_specs += [full(n_past, BR_W), full(n_past, BR_W), full(n_past, FX_HEADS)]
        args += list(past)
    return pl.pallas_call(
        functools.partial(_fox_kernel, n_past=n_past, tk=tk),
        grid=(b, nt),
        in_specs=in_specs,
        out_specs=[pl.BlockSpec((1, tq, BR_W), lambda i, j: (i, j, 0)),
                   pl.BlockSpec((1, tq, FX_HEADS), lambda i, j: (i, j, 0))],
        out_shape=[jax.ShapeDtypeStruct((b, t, BR_W), BF16),
                   jax.ShapeDtypeStruct((b, t, FX_HEADS), F32)],
        scratch_shapes=[pltpu.VMEM((n_past + t, FX_HEADS), F32),
                        pltpu.VMEM((1, FX_HEADS), F32), pltpu.VMEM((FX_HEADS, 1), F32),
                        pltpu.VMEM((BR_W, tq), F32), pltpu.VMEM((FX_HEADS, 2 * FX_HD, tq), BF16),
                        pltpu.VMEM((FX_HEADS, tq), F32), pltpu.VMEM((FX_HEADS, tq), F32)],
        compiler_params=_cparams("arbitrary", "arbitrary"),
        name="fox",
    )(*args)


def _hgrn_kernel(f_ref, i_ref, q_ref, g_ref, lb_ref, gn_ref, s0_ref, y_ref, s_ref, st_ref):
    tt = f_ref.shape[1]
    L = HG_CHUNK
    B = HG_BLOCK
    assert L == 4 * B

    @pl.when(pl.program_id(1) == 0)
    def _():
        for h in range(HG_HEADS):
            st_ref[h] = s0_ref[0, h].T

    tri = _tri(L, True)
    row = lax.broadcasted_iota(jnp.int32, (L, 1), 0)
    rb = lax.broadcasted_iota(jnp.int32, (L, L), 0) // B
    cb = lax.broadcasted_iota(jnp.int32, (L, L), 1) // B
    m_diag = _tri(L, True, B) > 0.5
    m_mid = ((rb == 1) & (cb == 0)) | ((rb == 3) & (cb == 2))
    m_far = (rb >= 2) & (cb < 2)
    lb = lb_ref[...]
    for c in range(tt // L):
        r0 = c * L
        z = f_ref[0, r0:r0 + L, :]
        logf = jnp.log(jnp.maximum(lb + (1.0 - lb) * _sigmoid(z), TINY))
        kk = (1.0 - lb) * _sigmoid(-z)
        qv = q_ref[0, r0:r0 + L, :]
        qq = qv * _sigmoid(qv)
        b = _dot_hi(tri, logf)
        r1, r2, r3, r4 = (b[j * B - 1:j * B, :] for j in (1, 2, 3, 4))
        start = jnp.where(row < B, 0.0, jnp.where(row < 2 * B, r1, jnp.where(row < 3 * B, r2, r3)))
        mid = jnp.where(row < 2 * B, r1, r3)
        bd = b - start
        ops = dict(
            qd=qq * jnp.exp(bd), kd=kk * jnp.exp(-bd),
            qm=qq * jnp.exp(jnp.minimum(b - mid, 0.0)), km=kk * jnp.exp(jnp.minimum(mid - b, 0.0)),
            qf=qq * jnp.exp(jnp.minimum(b - r2, 0.0)), kf=kk * jnp.exp(jnp.minimum(r2 - b, 0.0)),
            qs=qq * jnp.exp(b), ks=kk * jnp.exp(r4 - b))
        ops = {k: v.astype(BF16) for k, v in ops.items()}
        e4 = jnp.exp(r4)
        heads = [slice(h * HG_DK, (h + 1) * HG_DK) for h in range(HG_HEADS)]
        vs = [i_ref[0, r0:r0 + L, hl].astype(BF16) for hl in heads]
        a_d = [_dot_nt(ops['qd'][:, hl], ops['kd'][:, hl]) for hl in heads]
        a_m = [_dot_nt(ops['qm'][:, hl], ops['km'][:, hl]) for hl in heads]
        a_f = [_dot_nt(ops['qf'][:, hl], ops['kf'][:, hl]) for hl in heads]
        a = [(jnp.where(m_diag, a_d[h], 0.0) + jnp.where(m_mid, a_m[h], 0.0)
              + jnp.where(m_far, a_f[h], 0.0)).astype(BF16) for h in range(HG_HEADS)]
        s_old = [st_ref[h] for h in range(HG_HEADS)]
        o = [_dot(a[h], vs[h]) + _dot_nt(ops['qs'][:, heads[h]], s_old[h].astype(BF16)) for h in range(HG_HEADS)]
        for h, hl in enumerate(heads):
            st_ref[h] = e4[:, hl] * s_old[h] + _dot_tn(vs[h], ops['ks'][:, hl])
        for h, hl in enumerate(heads):
            rms = lax.rsqrt(jnp.mean(o[h] * o[h], -1, keepdims=True) + 1e-6)
            gv = g_ref[0, r0:r0 + L, hl]
            y_ref[0, r0:r0 + L, hl] = (o[h] * rms * gn_ref[:, hl] * (gv * _sigmoid(gv))).astype(y_ref.dtype)

    @pl.when(pl.program_id(1) == pl.num_programs(1) - 1)
    def _():
        for h in range(HG_HEADS):
            s_ref[0, h] = st_ref[h].T


def _hgrn(pa3, lb, gn, s0):
    b, t, _ = pa3.shape
    tt = min(t, 256)
    nt = t // tt
    seq = lambda blk: pl.BlockSpec((1, tt, BR_W), lambda i, j: (i, j, blk))
    vec = pl.BlockSpec((1, BR_W), lambda i, j: (0, 0))
    s_spec = pl.BlockSpec((1, HG_HEADS, HG_DK, HG_DK), lambda i, j: (i, 0, 0, 0))
    return pl.pallas_call(
        _hgrn_kernel,
        grid=(b, nt),
        in_specs=[seq(4), seq(5), seq(6), seq(7), vec, vec, s_spec],
        out_specs=[pl.BlockSpec((1, tt, BR_W), lambda i, j: (i, j, 0)), s_spec],
        out_shape=[jax.ShapeDtypeStruct((b, t, BR_W), BF16), jax.ShapeDtypeStruct(s0.shape, F32)],
        scratch_shapes=[pltpu.VMEM((HG_HEADS, HG_DK, HG_DK), F32)],
        compiler_params=_cparams("arbitrary", "arbitrary"),
        name="hgrn",
    )(pa3, pa3, pa3, pa3, lb, gn, s0)


RG_PAD = 8


def _rglru_kernel(x_ref, g_ref, cw_ref, cb_ref, wa_ref, ba_ref, wx_ref, bx_ref, lam_ref,
                  h0_ref, buf0_ref, y_ref, h_ref, buf_ref, xp_ref):
    tt = x_ref.shape[1]
    hist = RG_CONV - 1

    @pl.when(pl.program_id(1) == 0)
    def _():
        h_ref[...] = h0_ref[...]
        xp_ref[RG_PAD - hist:RG_PAD, :] = buf0_ref[0]

    x = x_ref[0]
    xp_ref[RG_PAD:RG_PAD + tt, :] = x
    u = cb_ref[...] + x * cw_ref[hist:hist + 1, :]
    for j in range(hist):
        u = u + xp_ref[RG_PAD - hist + j:RG_PAD - hist + j + tt, :] * cw_ref[j:j + 1, :]
    new_buf = xp_ref[RG_PAD + tt - hist:RG_PAD + tt, :]
    xp_ref[RG_PAD - hist:RG_PAD, :] = new_buf
    buf_ref[0] = new_buf

    ub = u.astype(BF16)
    r = _sigmoid(_dot(ub, wa_ref[...]) + ba_ref[...])
    ig = _sigmoid(_dot(ub, wx_ref[...]) + bx_ref[...])
    log_a = -RG_C * r * jax.nn.softplus(-lam_ref[...])
    a = jnp.exp(log_a)
    bt = jnp.sqrt(jnp.maximum(-jnp.tanh(log_a) * (a * a + 1.0), 0.0)) * (ig * u)
    row = lax.broadcasted_iota(jnp.int32, (tt, 1), 0)
    s = 1
    while s < tt:
        keep = row >= s
        a_sh = jnp.where(keep, pltpu.roll(a, s, 0), 1.0)
        b_sh = jnp.where(keep, pltpu.roll(bt, s, 0), 0.0)
        bt = a * b_sh + bt
        a = a * a_sh
        s *= 2
    h = a * h_ref[0] + bt
    h_ref[0] = h[tt - 1:tt, :]
    y_ref[0] = (h * _gelu(g_ref[0])).astype(y_ref.dtype)


def _rglru(pa3, cw, cb, wa, ba, wx, bx, lam, h0, buf0):
    b, t, _ = pa3.shape
    tt = min(t, 256)
    nt = t // tt
    seq = lambda blk: pl.BlockSpec((1, tt, RG_W), lambda i, j: (i, j, blk))
    vec = pl.BlockSpec((1, RG_W), lambda i, j: (0, 0))
    mat = pl.BlockSpec((RG_W, RG_W), lambda i, j: (0, 0))
    h_spec = pl.BlockSpec((1, 1, RG_W), lambda i, j: (i, 0, 0))
    buf_spec = pl.BlockSpec((1, RG_CONV - 1, RG_W), lambda i, j: (i, 0, 0))
    return pl.pallas_call(
        _rglru_kernel,
        grid=(b, nt),
        in_specs=[seq(8), seq(9), pl.BlockSpec((RG_CONV, RG_W), lambda i, j: (0, 0)), vec,
                  mat, vec, mat, vec, vec, h_spec, buf_spec],
        out_specs=[pl.BlockSpec((1, tt, RG_W), lambda i, j: (i, j, 0)), h_spec, buf_spec],
        out_shape=[jax.ShapeDtypeStruct((b, t, RG_W), BF16),
                   jax.ShapeDtypeStruct(h0.shape, F32), jax.ShapeDtypeStruct(buf0.shape, F32)],
        scratch_shapes=[pltpu.VMEM((RG_PAD + tt, RG_W), F32)],
        compiler_params=_cparams("arbitrary", "arbitrary"),
        name="rglru",
    )(pa3, pa3, cw, cb, wa, ba, wx, bx, lam, h0, buf0)


def _merge_kernel(x_ref, y0_ref, y1_ref, y2_ref, y3_ref, wmg_ref, bmg_ref, wbr_ref, wout_ref,
                  g_ref, b_ref, o_ref, *, alpha):
    x = x_ref[...]
    xb = x.astype(BF16)
    mix = None
    for m, y_ref in enumerate((y0_ref, y1_ref, y2_ref, y3_ref)):
        gate = _sigmoid(_dot(xb, wmg_ref[m]) + bmg_ref[m])
        term = gate * _dot(y_ref[...], wbr_ref[m])
        mix = term if mix is None else mix + term
    o_ref[...] = _layer_norm(alpha * x + _dot(mix.astype(BF16), wout_ref[...]), g_ref[...], b_ref[...])


def _merge(x2d, ys, wmg, bmg, wbr, wout, g, bb, alpha):
    m = x2d.shape[0]
    tm = min(m, 256)
    row = lambda width: pl.BlockSpec((tm, width), lambda i: (i, 0))
    c2 = lambda shape: pl.BlockSpec(shape, lambda i: (0, 0))
    c3 = lambda shape: pl.BlockSpec(shape, lambda i: (0, 0, 0))
    return pl.pallas_call(
        functools.partial(_merge_kernel, alpha=alpha),
        grid=(m // tm,),
        in_specs=[row(D_MODEL)] + [row(BR_W)] * N_BRANCH +
                 [c3((N_BRANCH, D_MODEL, D_MODEL)), c3((N_BRANCH, 1, D_MODEL)),
                  c3((N_BRANCH, BR_W, D_MODEL)), c2((D_MODEL, D_MODEL)),
                  c2((1, D_MODEL)), c2((1, D_MODEL))],
        out_specs=row(D_MODEL),
        out_shape=jax.ShapeDtypeStruct((m, D_MODEL), F32),
        compiler_params=_cparams("arbitrary"),
        name="merge",
    )(x2d, *ys, wmg, bmg, wbr, wout, g, bb)


FF_PAD = 8


def _ffn_kernel(x_ref, wg_ref, wu_ref, cw_ref, cb_ref, wd_ref, g_ref, b_ref, buf0_ref,
                o_ref, buf_ref, gp_ref, *, alpha):
    tt = x_ref.shape[1]
    hist = FFN_CONV - 1

    @pl.when(pl.program_id(1) == 0)
    def _():
        gp_ref[FF_PAD - hist:FF_PAD, :] = buf0_ref[0]

    x = x_ref[0]
    xb = x.astype(BF16)
    acc = None
    for c0 in range(0, D_FF, COL_BLK):
        cols = slice(c0, c0 + COL_BLK)
        gp = _dot(xb, wg_ref[:, cols])
        gp_ref[FF_PAD:FF_PAD + tt, cols] = gp
        gc = cb_ref[:, cols] + gp * cw_ref[hist:hist + 1, cols]
        for j in range(hist):
            gc = gc + gp_ref[FF_PAD - hist + j:FF_PAD - hist + j + tt, cols] * cw_ref[j:j + 1, cols]
        hmid = (_gelu(gc) * _dot(xb, wu_ref[:, cols])).astype(BF16)
        part = _dot(hmid, wd_ref[cols, :])
        acc = part if acc is None else acc + part
    new_buf = gp_ref[FF_PAD + tt - hist:FF_PAD + tt, :]
    gp_ref[FF_PAD - hist:FF_PAD, :] = new_buf
    buf_ref[0] = new_buf
    o_ref[0] = _layer_norm(alpha * x + acc, g_ref[...], b_ref[...])


def _ffn(x3, wg, wu, cw, cb, wd, g, bb, buf0, alpha):
    b, t, _ = x3.shape
    tt = min(t, 256)
    nt = t // tt
    c2 = lambda shape: pl.BlockSpec(shape, lambda i, j: (0, 0))
    buf_spec = pl.BlockSpec((1, FFN_CONV - 1, D_FF), lambda i, j: (i, 0, 0))
    seq = pl.BlockSpec((1, tt, D_MODEL), lambda i, j: (i, j, 0))
    return pl.pallas_call(
        functools.partial(_ffn_kernel, alpha=alpha),
        grid=(b, nt),
        in_specs=[seq, c2((D_MODEL, D_FF)), c2((D_MODEL, D_FF)), c2((FFN_CONV, D_FF)), c2((1, D_FF)),
                  c2((D_FF, D_MODEL)), c2((1, D_MODEL)), c2((1, D_MODEL)), buf_spec],
        out_specs=[seq, buf_spec],
        out_shape=[jax.ShapeDtypeStruct(x3.shape, F32), jax.ShapeDtypeStruct(buf0.shape, F32)],
        scratch_shapes=[pltpu.VMEM((FF_PAD + tt, D_FF), F32)],
        compiler_params=_cparams("arbitrary", "arbitrary"),
        name="ffn",
    )(x3, wg, wu, cw, cb, wd, g, bb, buf0)


def _prep_layer(l, w_in, b_in, ml_norm_g, hg_norm_g, lbs, rg_conv_w, rg_conv_b, rg_w_a, rg_b_a,
                rg_w_x, rg_b_x, rg_lambda, w_mg, b_mg, w_br, w_out, ln1_g, ln1_b,
                w_ff_gate, w_ff_up, ff_conv_w, ff_conv_b, w_ff_down, ln2_g, ln2_b):
    offs = np.concatenate([[0], np.cumsum(PROJ_SIZES)])
    cols = lambda segs: np.concatenate([np.arange(offs[s], offs[s + 1]) for s in segs])
    main, small, kv = cols(MAIN_SEGS), cols(SMALL_SEGS), cols((7, 8))
    wl, bl = w_in[l], b_in[l]
    zpad = jnp.zeros((D_MODEL, SMALL_W - N_SMALL), F32)
    w = jnp.concatenate([wl[:, main], wl[:, small], zpad, wl[:, kv]], axis=1).astype(BF16)
    bvec = jnp.concatenate([bl[main], bl[small], jnp.zeros((SMALL_W - N_SMALL,), F32), bl[kv]])[None, :]
    wst = wl[:, small].T.astype(BF16)
    bst = bl[small][:, None]

    def block_diag(wb):
        eye = jnp.eye(RG_BLOCKS, dtype=F32)
        return jnp.einsum('nde,nm->ndme', wb, eye).reshape(RG_W, RG_W).astype(BF16)

    row = lambda v: v[None, :].astype(F32)
    return dict(
        w=w, b=bvec, wst=wst, bst=bst, ml_g=row(ml_norm_g[l]), hg_g=row(hg_norm_g[l]), lb=row(lbs[l]),
        rg_cw=rg_conv_w[l], rg_cb=row(rg_conv_b[l]), rg_wa=block_diag(rg_w_a[l]), rg_ba=row(rg_b_a[l]),
        rg_wx=block_diag(rg_w_x[l]), rg_bx=row(rg_b_x[l]), rg_lam=row(rg_lambda[l]),
        wmg=w_mg[l].astype(BF16), bmg=b_mg[l][:, None, :], wbr=w_br[l].astype(BF16),
        wout=w_out[l].astype(BF16), ln1_g=row(ln1_g[l]), ln1_b=row(ln1_b[l]),
        wg=w_ff_gate[l].astype(BF16), wu=w_ff_up[l].astype(BF16), ff_cw=ff_conv_w[l],
        ff_cb=row(ff_conv_b[l]), wd=w_ff_down[l].astype(BF16), ln2_g=row(ln2_g[l]), ln2_b=row(ln2_b[l]))


def _trunk_layer(x, p, fox_past, ml_c, ml_n, ml_m, hg_s, rg_h, rg_buf, ff_buf, alpha):
    b, t, _ = x.shape
    assert t % 64 == 0 and (t <= 256 or t % 256 == 0)
    x2d = x.reshape(b * t, D_MODEL)
    pa3, k32, v32, kb, vt, st = _proj(x, p['w'], p['b'], p['wst'], p['bst'])
    y_ml, ml_c, ml_n, ml_m = _mlstm(pa3, st, ml_c, ml_n, ml_m.reshape(b, 1, ML_HEADS), p['ml_g'])
    past = None
    if fox_past is not None:
        k_past, v_past, logf_past = fox_past
        n_past = k_past.shape[1]
        past = (k_past.reshape(b, n_past, BR_W), v_past.reshape(b, n_past, BR_W),
                logf_past)
    y_fx, f_log = _fox(pa3, st, kb, vt, past)
    y_hg, hg_s = _hgrn(pa3, p['lb'], p['hg_g'], hg_s)
    y_rg, rg_h, rg_buf = _rglru(pa3, p['rg_cw'], p['rg_cb'], p['rg_wa'], p['rg_ba'], p['rg_wx'], p['rg_bx'],
                                p['rg_lam'], rg_h.reshape(b, 1, RG_W), rg_buf)
    ys = [y.reshape(b * t, BR_W) for y in (y_ml, y_fx, y_hg, y_rg)]
    x1 = _merge(x2d, ys, p['wmg'], p['bmg'], p['wbr'], p['wout'], p['ln1_g'], p['ln1_b'], alpha)
    x2, ff_buf = _ffn(x1.reshape(b, t, D_MODEL), p['wg'], p['wu'], p['ff_cw'], p['ff_cb'], p['wd'],
                      p['ln2_g'], p['ln2_b'], ff_buf, alpha)
    state = (k32.reshape(b, t, FX_HEADS, FX_HD), v32.reshape(b, t, FX_HEADS, FX_HD), f_log,
             ml_c, ml_n, ml_m.reshape(b, ML_HEADS), hg_s, rg_h.reshape(b, RG_W), rg_buf, ff_buf)
    return x2, state


def kernel(x_prompt, x_sample, cache_fox_k, cache_fox_v, cache_fox_logf, state_mlstm_c, state_mlstm_n,
           state_mlstm_m, state_hgrn_s, state_rglru_h, state_rglru_conv, state_ffn_conv,
           w_in, b_in, ml_norm_g, hg_norm_g, hg_lb_logits, rg_conv_w, rg_conv_b, rg_w_a, rg_b_a,
           rg_w_x, rg_b_x, rg_lambda, w_mg, b_mg, w_br, w_out, ln1_g, ln1_b,
           w_ff_gate, w_ff_up, ff_conv_w, ff_conv_b, w_ff_down, ln2_g, ln2_b):
    depth = w_in.shape[0]
    alpha = (2 * depth) ** 0.25
    pl_soft = jax.nn.softmax(hg_lb_logits.astype(F32), axis=0)
    lbs = jnp.cumsum(pl_soft, axis=0) - pl_soft[0]
    bp = x_prompt.shape[0]
    yp, ys = x_prompt, x_sample
    p_new, s_new = [], []
    for l in range(depth):
        p = _prep_layer(l, w_in, b_in, ml_norm_g, hg_norm_g, lbs, rg_conv_w, rg_conv_b, rg_w_a, rg_b_a,
                        rg_w_x, rg_b_x, rg_lambda, w_mg, b_mg, w_br, w_out, ln1_g, ln1_b,
                        w_ff_gate, w_ff_up, ff_conv_w, ff_conv_b, w_ff_down, ln2_g, ln2_b)
        z = lambda *shape: jnp.zeros((bp,) + shape, F32)
        yp, st_p = _trunk_layer(yp, p, None, z(ML_HEADS, ML_DK, ML_DV), z(ML_HEADS, ML_DK), z(ML_HEADS),
                                z(HG_HEADS, HG_DK, HG_DK), z(RG_W), z(RG_CONV - 1, RG_W),
                                z(FFN_CONV - 1, D_FF), alpha)
        p_new.append(st_p)
        ys, st_s = _trunk_layer(ys, p, (cache_fox_k[l], cache_fox_v[l], cache_fox_logf[l]),
                                state_mlstm_c[l], state_mlstm_n[l], state_mlstm_m[l], state_hgrn_s[l],
                                state_rglru_h[l], state_rglru_conv[l], state_ffn_conv[l], alpha)
        s_new.append(st_s)
    n_st = len(p_new[0])
    p_out = [jnp.stack([st[j] for st in p_new]) for j in range(n_st)]
    s_out = [jnp.stack([st[j] for st in s_new]) for j in range(n_st)]
    return (yp, ys, *p_out, *s_out)
```

```python
import functools

import numpy as np
import jax
import jax.numpy as jnp
from jax import lax
from jax.experimental import pallas as pl
from jax.experimental.pallas import tpu as pltpu

F32 = jnp.float32
BF16 = jnp.bfloat16

D_MODEL = 1024
BR_W = D_MODEL // 2
ML_HEADS = 4
ML_DV = BR_W // ML_HEADS
ML_DK = ML_DV // 2
ML_CHUNK = 128
FX_HD = 64
FX_HEADS = BR_W // FX_HD
FX_PAST_TILE = 128
FX_KEY_TILE = 256
LOG2E = 1.4426950408889634
HG_HEADS = 4
HG_DK = BR_W // HG_HEADS
HG_BLOCK = 16
HG_CHUNK = 64
RG_W = BR_W
RG_BLOCKS = 8
RG_BD = RG_W // RG_BLOCKS
RG_CONV = 4
RG_C = 8.0
D_FF = 2 * D_MODEL
FFN_CONV = 3
N_BRANCH = 4
PROJ_SIZES = (ML_HEADS * ML_DK, ML_HEADS * ML_DK, BR_W, BR_W, ML_HEADS, ML_HEADS,
              BR_W, BR_W, BR_W, FX_HEADS,
              BR_W, BR_W, BR_W, BR_W,
              RG_W, RG_W)
NEG_BIG = -1e30
TINY = 1e-30

LANE = 128
SUBLANE = 8
SMALL_W = LANE
N_SMALL = 2 * ML_HEADS + FX_HEADS
MAIN_SEGS = (0, 1, 2, 3, 6, 10, 11, 12, 13)
SMALL_SEGS = (4, 5, 9)
RG_SEGS = (14, 15)
KV_SEGS = (7, 8)
N_MAIN = sum(PROJ_SIZES[i] for i in MAIN_SEGS)
PA_W = N_MAIN + SMALL_W
RG_COL0 = PA_W
KV_COL0 = RG_COL0 + 2 * RG_W
COL_BLK = 512
DENSE_ROWS = 512
VMEM_LIMIT = 56 * 1024 * 1024


def _cparams(*sem):
    return pltpu.CompilerParams(dimension_semantics=sem, vmem_limit_bytes=VMEM_LIMIT)


def _dot(a, b):
    return jnp.dot(a, b, preferred_element_type=F32)


def _dot_nt(a, b):
    return lax.dot_general(a, b, (((1,), (1,)), ((), ())), preferred_element_type=F32)


def _dot_tn(a, b):
    return lax.dot_general(a, b, (((0,), (0,)), ((), ())), preferred_element_type=F32)


def _split3(x):
    hi = x.astype(BF16)
    r1 = x - hi.astype(F32)
    mid = r1.astype(BF16)
    lo = (r1 - mid.astype(F32)).astype(BF16)
    return hi, mid, lo


def _tri_dot(tri, x):
    t = tri.astype(BF16)
    hi, mid, lo = _split3(x)
    return (_dot(t, lo) + _dot(t, mid)) + _dot(t, hi)


def _dot_tri(x, tri):
    t = tri.astype(BF16)
    hi, mid, lo = _split3(x)
    return (_dot(lo, t) + _dot(mid, t)) + _dot(hi, t)


def _tri(n, lower, block=None):
    r = lax.broadcasted_iota(jnp.int32, (n, n), 0)
    c = lax.broadcasted_iota(jnp.int32, (n, n), 1)
    m = (r >= c) if lower else (r <= c)
    if block is not None:
        m = m & ((r // block) == (c // block))
    return m.astype(F32)


def _dot_hi_nt(a, b):
    return lax.dot_general(a, b, (((1,), (1,)), ((), ())), precision=lax.Precision.HIGHEST,
                           preferred_element_type=F32)


_log_sigmoid = jax.nn.log_sigmoid
_sigmoid = jax.nn.sigmoid
_gelu = jax.nn.gelu


def _layer_norm(v, g, b):
    mu = jnp.mean(v, -1, keepdims=True)
    c = v - mu
    var = jnp.mean(c * c, -1, keepdims=True)
    return c * lax.rsqrt(var + 1e-5) * g + b


N_RG_IN = 9


def _proj_kernel(*refs, t_minor, n_alias):
    x_ref, w_ref, b_ref, wst_ref, bst_ref = refs[:5]
    rg_in = refs[5:5 + N_RG_IN]
    (pa_ref, k32_ref, v32_ref, kb_ref, vb_ref, st_ref,
     yrg_ref, rgh_ref, rgbuf_ref, xp_ref) = refs[5 + N_RG_IN + n_alias:]
    xb = x_ref[0].astype(BF16)
    col = lambda c0, width: _dot(xb, w_ref[:, c0:c0 + width]) + b_ref[:, c0:c0 + width]
    _rglru_tile(col(RG_COL0, RG_W), col(RG_COL0 + RG_W, RG_W), *rg_in, yrg_ref, rgh_ref, rgbuf_ref, xp_ref)
    for c0 in range(0, PA_W, COL_BLK):
        width = min(COL_BLK, PA_W - c0)
        pa_ref[0, :, c0:c0 + width] = col(c0, width)
    k = col(KV_COL0, BR_W)
    kb_ref[0] = k.astype(BF16)
    v = col(KV_COL0 + BR_W, BR_W)
    vt = v.T
    vb_ref[0] = vt.astype(BF16)
    k32_ref[0, 0] = k.T if t_minor else k
    v32_ref[0, 0] = vt if t_minor else v
    st_ref[0] = _dot_nt(wst_ref[...], xb) + bst_ref[...]


def _proj(x3, w, b, wst, bst, rg, layer, depth, t_minor, kv_bufs):
    bsz, t, _ = x3.shape
    tt = min(t, 256)
    nw = w.shape[1]
    seq = lambda width: pl.BlockSpec((1, tt, width), lambda i, j: (i, j, 0))
    const = lambda shape: pl.BlockSpec(shape, lambda i, j: (0, 0))
    sds = lambda width, dt: jax.ShapeDtypeStruct((bsz, t, width), dt)
    vec = const((1, RG_W))
    mat = const((RG_W, RG_W))
    h_spec = pl.BlockSpec((1, 1, RG_W), lambda i, j: (i, 0, 0))
    buf_spec = pl.BlockSpec((1, RG_CONV - 1, RG_W), lambda i, j: (i, 0, 0))
    rg_specs = [const((RG_CONV, RG_W)), vec, mat, vec, mat, vec, vec, h_spec, buf_spec]
    assert len(rg) == N_RG_IN
    if t_minor:
        kv_spec = pl.BlockSpec((1, 1, BR_W, tt), lambda i, j: (layer, i, 0, j))
        kv_sds = jax.ShapeDtypeStruct((depth, bsz, BR_W, t), F32)
    else:
        kv_spec = pl.BlockSpec((1, 1, tt, BR_W), lambda i, j: (layer, i, j, 0))
        kv_sds = jax.ShapeDtypeStruct((depth, bsz, t, BR_W), F32)
    n_alias = len(kv_bufs)
    return pl.pallas_call(
        functools.partial(_proj_kernel, t_minor=t_minor, n_alias=n_alias),
        grid=(bsz, t // tt),
        in_specs=[seq(D_MODEL), const((D_MODEL, nw)), const((1, nw)),
                  const((2 * 8, D_MODEL)), const((2 * 8, 1))] + rg_specs
                 + [pl.BlockSpec(memory_space=pl.ANY)] * n_alias,
        out_specs=[seq(PA_W), kv_spec, kv_spec, seq(BR_W),
                   pl.BlockSpec((1, BR_W, tt), lambda i, j: (i, 0, j)),
                   pl.BlockSpec((1, 2 * 8, tt), lambda i, j: (i, 0, j)),
                   seq(RG_W), h_spec, buf_spec],
        out_shape=[sds(PA_W, F32), kv_sds, kv_sds, sds(BR_W, BF16),
                   jax.ShapeDtypeStruct((bsz, BR_W, t), BF16),
                   jax.ShapeDtypeStruct((bsz, 2 * 8, t), F32),
                   sds(RG_W, BF16), jax.ShapeDtypeStruct((bsz, 1, RG_W), F32),
                   jax.ShapeDtypeStruct((bsz, RG_CONV - 1, RG_W), F32)],
        input_output_aliases={5 + N_RG_IN + a: 1 + a for a in range(n_alias)},
        scratch_shapes=[pltpu.VMEM((RG_PAD + tt, RG_W), F32)],
        compiler_params=_cparams("arbitrary", "arbitrary"),
        name="proj",
    )(x3, w, b, wst, bst, *rg, *kv_bufs)


def _mlstm_kernel(q_ref, k_ref, v_ref, o_ref, sm_ref, st_ref, c0_ref, n0_ref, m0_ref, g_ref,
                  y_ref, c_ref, n_ref, m_ref, ct_ref):
    tt = q_ref.shape[1]
    L = min(tt, ML_CHUNK)
    H = range(ML_HEADS)

    @pl.when(pl.program_id(1) == 0)
    def _():
        for h in H:
            ct_ref[h] = c0_ref[0, h].T
        n_ref[...] = n0_ref[...]
        m_ref[...] = m0_ref[...]

    tri_l = _tri(L, True)
    tri_u = _tri(L, False)
    causal_t = tri_u > 0.5
    for c in range(tt // L):
        r0 = c * L
        sm = sm_ref[0, r0:r0 + L, :]
        st = st_ref[0, :, r0:r0 + L]
        i_col, lf_col = sm[:, 0:ML_HEADS], _log_sigmoid(sm[:, ML_HEADS:2 * ML_HEADS])
        lf_row = _log_sigmoid(st[ML_HEADS:2 * ML_HEADS, :])
        a_cols = i_col - _tri_dot(tri_l, lf_col)
        fcum_row = _dot_tri(lf_row, tri_u)
        m_all = m_ref[0]
        dk = [slice(h * ML_DK, (h + 1) * ML_DK) for h in H]
        dv = [slice(h * ML_DV, (h + 1) * ML_DV) for h in H]
        qf = [q_ref[0, r0:r0 + L, dk[h]] for h in H]
        qb = [x.astype(BF16) for x in qf]
        kf = [k_ref[0, r0:r0 + L, dk[h]] * (ML_DK ** -0.5) for h in H]
        vb = [v_ref[0, r0:r0 + L, dv[h]].astype(BF16) for h in H]
        ct = [ct_ref[h] for h in H]
        nst = [n_ref[0, h:h + 1, :] for h in H]
        s_t = [_dot_nt(kf[h].astype(BF16), qb[h]) for h in H]
        qc_t = [_dot_nt(ct[h].astype(BF16), qb[h]) for h in H]
        qn = [_dot_hi_nt(nst[h], qf[h]) for h in H]
        w_t, g, m_t = [], [], []
        for h in H:
            fr = fcum_row[h:h + 1, :]
            d_t = jnp.where(causal_t, fr + a_cols[:, h:h + 1], NEG_BIG)
            prev = fr + m_all[:, h:h + 1]
            m_t.append(jnp.maximum(prev, jnp.max(d_t, 0, keepdims=True)))
            w_t.append(jnp.exp(d_t - m_t[h]) * s_t[h])
            g.append(jnp.exp(prev - m_t[h]))
        num_t = [_dot_tn(vb[h], w_t[h].astype(BF16)) + g[h] * qc_t[h] for h in H]
        ks = []
        for h in H:
            m_last, fr_last = m_t[h][:, L - 1:L], fcum_row[h:h + 1, L - 1:L]
            ks.append(jnp.exp(fr_last + a_cols[:, h:h + 1] - m_last) * kf[h])
        dct = [_dot_tn(vb[h], ks[h].astype(BF16)) for h in H]
        for h in H:
            g_last = g[h][:, L - 1:L]
            ct_ref[h] = g_last * ct[h] + dct[h]
            n_ref[0, h:h + 1, :] = g_last * nst[h] + jnp.sum(ks[h], 0, keepdims=True)
        for h in H:
            den = jnp.sum(w_t[h], 0, keepdims=True) + g[h] * qn[h]
            h_t = num_t[h] / jnp.maximum(jnp.abs(den), jnp.exp(-m_t[h]))
            mu = jnp.mean(h_t, 0, keepdims=True)
            hc = h_t - mu
            var = jnp.mean(hc * hc, 0, keepdims=True)
            hn = (hc * lax.rsqrt(var + 1e-5)).T
            yv = _sigmoid(o_ref[0, r0:r0 + L, dv[h]]) * (hn * g_ref[:, dv[h]])
            y_ref[0, r0:r0 + L, dv[h]] = yv.astype(y_ref.dtype)
        head = lax.broadcasted_iota(jnp.int32, (1, ML_HEADS), 1)
        m_vec = m_t[0][:, L - 1:L]
        for h in range(1, ML_HEADS):
            m_vec = jnp.where(head == h, m_t[h][:, L - 1:L], m_vec)
        m_ref[0] = m_vec

    @pl.when(pl.program_id(1) == pl.num_programs(1) - 1)
    def _():
        for h in H:
            c_ref[0, h] = ct_ref[h].T


def _mlstm(pa3, st, c0, n0, m0, g):
    b, t, _ = pa3.shape
    tt = min(t, 256)
    nt = t // tt
    seq = lambda width, blk: pl.BlockSpec((1, tt, width), lambda i, j: (i, j, blk))
    st_spec = pl.BlockSpec((1, 2 * 8, tt), lambda i, j: (i, 0, j))
    c_spec = pl.BlockSpec((1, ML_HEADS, ML_DK, ML_DV), lambda i, j: (i, 0, 0, 0))
    n_spec = pl.BlockSpec((1, ML_HEADS, ML_DK), lambda i, j: (i, 0, 0))
    m_spec = pl.BlockSpec((1, 1, ML_HEADS), lambda i, j: (i, 0, 0))
    return pl.pallas_call(
        _mlstm_kernel,
        grid=(b, nt),
        in_specs=[seq(256, 0), seq(256, 1), seq(512, 1), seq(512, 2), seq(SMALL_W, N_MAIN // SMALL_W),
                  st_spec, c_spec, n_spec, m_spec, pl.BlockSpec((1, BR_W), lambda i, j: (0, 0))],
        out_specs=[pl.BlockSpec((1, tt, BR_W), lambda i, j: (i, j, 0)), c_spec, n_spec, m_spec],
        out_shape=[jax.ShapeDtypeStruct((b, t, BR_W), BF16),
                   jax.ShapeDtypeStruct(c0.shape, F32), jax.ShapeDtypeStruct(n0.shape, F32),
                   jax.ShapeDtypeStruct(m0.shape, F32)],
        scratch_shapes=[pltpu.VMEM((ML_HEADS, ML_DV, ML_DK), F32)],
        compiler_params=_cparams("arbitrary", "arbitrary"),
        name="mlstm",
    )(pa3, pa3, pa3, pa3, pa3, st, c0, n0, m0, g)


def _fox_kernel(*refs, n_past, tk):
    if n_past:
        (q_ref, sm_ref, st_ref, kb_ref, vt_ref, kp_ref, vp_ref, lfp_ref,
         y_ref, lf_ref, ccol_ref, crun_ref, rrun_ref, yt_ref, qm_ref, m_ref, l_ref) = refs
    else:
        (q_ref, sm_ref, st_ref, kb_ref, vt_ref,
         y_ref, lf_ref, ccol_ref, crun_ref, rrun_ref, yt_ref, qm_ref, m_ref, l_ref) = refs
    tq = q_ref.shape[1]
    tkp = FX_PAST_TILE
    single = kb_ref.shape[1] == tq
    t_idx = 0 if single else pl.program_id(1)
    tri_l = _tri(tq, True)
    tri_u = _tri(tq, False)

    def init():
        crun_ref[...] = jnp.zeros_like(crun_ref)
        if n_past:
            tri_p = _tri(tkp, True)
            for j in range(n_past // tkp):
                cc_p = crun_ref[...] + _tri_dot(tri_p, lfp_ref[0, j * tkp:(j + 1) * tkp, :])
                ccol_ref[j * tkp:(j + 1) * tkp, :] = cc_p * LOG2E
                crun_ref[...] = cc_p[tkp - 1:tkp, :]
        eye = (lax.broadcasted_iota(jnp.int32, (FX_HEADS, FX_HEADS), 0)
               == lax.broadcasted_iota(jnp.int32, (FX_HEADS, FX_HEADS), 1))
        rrun_ref[...] = jnp.sum(jnp.where(eye, crun_ref[...], 0.0), -1, keepdims=True)

    if single:
        init()
    else:
        pl.when(t_idx == 0)(init)

    lf_col = _log_sigmoid(sm_ref[0, :, 2 * ML_HEADS:N_SMALL])
    lf_row = _log_sigmoid(st_ref[0, 2 * ML_HEADS:N_SMALL, :])
    lf_ref[0] = lf_row
    cc = crun_ref[...] + _tri_dot(tri_l, lf_col)
    cr = rrun_ref[...] + _dot_tri(lf_row, tri_u)
    crun_ref[...] = cc[tq - 1:tq, :]
    rrun_ref[...] = cr[:, tq - 1:tq]
    new0 = n_past if single else pl.multiple_of(n_past + t_idx * tq, tq)
    ccol_ref[pl.ds(new0, tq), :] = cc * LOG2E
    cq_rows = cr * LOG2E

    n_sub = tq // tk
    krow = lax.broadcasted_iota(jnp.int32, (tk, tq), 0)
    qcol = lax.broadcasted_iota(jnp.int32, (tk, tq), 1)
    first_head = lax.broadcasted_iota(jnp.int32, (2 * FX_HD, 1), 0) < FX_HD
    qt = (q_ref[0] * (FX_HD ** -0.5 * LOG2E)).T

    for h in range(FX_HEADS):
        qp = qt[(h // 2) * 2 * FX_HD:(h // 2 + 1) * 2 * FX_HD]
        qm_ref[h] = (jnp.where(first_head, qp, 0.0) if h % 2 == 0 else jnp.where(first_head, 0.0, qp)).astype(BF16)
    yt_ref[...] = jnp.zeros_like(yt_ref)
    m_ref[...] = jnp.full(m_ref.shape, NEG_BIG, F32)
    l_ref[...] = jnp.zeros_like(l_ref)

    def tile(k_pairs, vt_pairs, ck, mask, k_is_transposed=False):
        qk = _dot_tn if k_is_transposed else _dot
        ss = [qk(k_pairs[h // 2], qm_ref[h]) for h in range(FX_HEADS)]
        ps, alphas = [], []
        for h in range(FX_HEADS):
            m_, l_ = m_ref[h:h + 1, :], l_ref[h:h + 1, :]
            u = ss[h] - ck[:, h:h + 1]
            if mask is not None:
                u = jnp.where(mask, u, NEG_BIG)
            cq = cq_rows[h:h + 1, :]
            mn = jnp.maximum(m_, jnp.max(u, 0, keepdims=True) + cq)
            pe = jnp.exp2(u + (cq - mn))
            a = jnp.exp2(m_ - mn)
            m_ref[h:h + 1, :] = mn
            l_ref[h:h + 1, :] = a * l_ + jnp.sum(pe, 0, keepdims=True)
            ps.append(pe.astype(BF16))
            alphas.append(a)
        for h in range(FX_HEADS):
            p, e = divmod(h, 2)
            hrows = slice(h * FX_HD, (h + 1) * FX_HD)
            yt_ref[hrows, :] = alphas[h] * yt_ref[hrows, :] + _dot(vt_pairs[p][e * FX_HD:(e + 1) * FX_HD, :], ps[h])

    pair_lanes = [slice(p * 2 * FX_HD, (p + 1) * 2 * FX_HD) for p in range(FX_HEADS // 2)]
    if n_past:
        def past_body(j, _):
            k0 = pl.multiple_of(j * tkp, tkp)
            ks = [kp_ref[0, 0, pp, pl.ds(k0, tkp)].astype(BF16) for pp in pair_lanes]
            vs = [vp_ref[0, 0, pp, pl.ds(k0, tkp)].astype(BF16) for pp in pair_lanes]
            tile(ks, vs, ccol_ref[pl.ds(k0, tkp), :], None, k_is_transposed=True)
            return 0
        lax.fori_loop(0, n_past // tkp, past_body, 0)

    if not single:
        def new_body(j, _):
            k0 = pl.multiple_of(j * tk, tk)
            ks = [kb_ref[0, pl.ds(k0, tk), pp] for pp in pair_lanes]
            vs = [vt_ref[0, pp, pl.ds(k0, tk)] for pp in pair_lanes]
            tile(ks, vs, ccol_ref[pl.ds(pl.multiple_of(n_past + j * tk, tk), tk), :], None)
            return 0
        lax.fori_loop(0, t_idx * n_sub, new_body, 0)

    for i in range(n_sub):
        k0 = i * tk if single else pl.multiple_of(t_idx * tq + i * tk, tk)
        ks = [kb_ref[0, pl.ds(k0, tk), pp] for pp in pair_lanes]
        vs = [vt_ref[0, pp, pl.ds(k0, tk)] for pp in pair_lanes]
        tile(ks, vs, ccol_ref[pl.ds(n_past + k0, tk), :], krow + i * tk <= qcol)
    for h in range(FX_HEADS):
        hrows = slice(h * FX_HD, (h + 1) * FX_HD)
        yt_ref[hrows, :] = yt_ref[hrows, :] / l_ref[h:h + 1, :]
    y_ref[0] = yt_ref[...].T.astype(y_ref.dtype)


def _fox(pa3, st, kb3, vt3, past=None, layer=0):
    b, t, _ = pa3.shape
    tq = min(t, 256)
    tk = min(tq, FX_KEY_TILE)
    nt = t // tq
    n_past = 0 if past is None else past[2].shape[1]
    seq = lambda width, blk: pl.BlockSpec((1, tq, width), lambda i, j: (i, j, blk))
    full = lambda n, width: pl.BlockSpec((1, n, width), lambda i, j: (i, 0, 0))
    in_specs = [seq(BR_W, 3), seq(SMALL_W, N_MAIN // SMALL_W),
                pl.BlockSpec((1, 2 * 8, tq), lambda i, j: (i, 0, j)),
                full(t, BR_W), full(BR_W, t)]
    args = [pa3, pa3, st, kb3, vt3]
    assert n_past % FX_PAST_TILE == 0
    if n_past:
        cache = pl.BlockSpec((1, 1, BR_W, n_past), lambda i, j: (layer, i, 0, 0))
        in_specs += [cache, cache, full(n_past, FX_HEADS)]
        args += list(past)
    return pl.pallas_call(
        functools.partial(_fox_kernel, n_past=n_past, tk=tk),
        grid=(b, nt),
        in_specs=in_specs,
        out_specs=[pl.BlockSpec((1, tq, BR_W), lambda i, j: (i, j, 0)),
                   pl.BlockSpec((1, FX_HEADS, tq), lambda i, j: (i, 0, j))],
        out_shape=[jax.ShapeDtypeStruct((b, t, BR_W), BF16),
                   jax.ShapeDtypeStruct((b, FX_HEADS, t), F32)],
        scratch_shapes=[pltpu.VMEM((n_past + t, FX_HEADS), F32),
                        pltpu.VMEM((1, FX_HEADS), F32), pltpu.VMEM((FX_HEADS, 1), F32),
                        pltpu.VMEM((BR_W, tq), F32), pltpu.VMEM((FX_HEADS, 2 * FX_HD, tq), BF16),
                        pltpu.VMEM((FX_HEADS, tq), F32), pltpu.VMEM((FX_HEADS, tq), F32)],
        compiler_params=_cparams("arbitrary", "arbitrary"),
        name="fox",
    )(*args)


def _hgrn_kernel(f_ref, i_ref, q_ref, g_ref, lb_ref, gn_ref, s0_ref, y_ref, s_ref, st_ref):
    tt = f_ref.shape[1]
    L = HG_CHUNK
    B = HG_BLOCK
    assert L == 4 * B

    @pl.when(pl.program_id(1) == 0)
    def _():
        for h in range(HG_HEADS):
            st_ref[h] = s0_ref[0, h].T

    tri = _tri(L, True)
    row = lax.broadcasted_iota(jnp.int32, (L, 1), 0)
    rb = lax.broadcasted_iota(jnp.int32, (L, L), 0) // B
    cb = lax.broadcasted_iota(jnp.int32, (L, L), 1) // B
    m_diag = _tri(L, True, B) > 0.5
    m_mid = ((rb == 1) & (cb == 0)) | ((rb == 3) & (cb == 2))
    m_far = (rb >= 2) & (cb < 2)
    lb = lb_ref[...]
    for c in range(tt // L):
        r0 = c * L
        z = f_ref[0, r0:r0 + L, :]
        logf = jnp.log(jnp.maximum(lb + (1.0 - lb) * _sigmoid(z), TINY))
        kk = (1.0 - lb) * _sigmoid(-z)
        qv = q_ref[0, r0:r0 + L, :]
        qq = qv * _sigmoid(qv)
        b = _tri_dot(tri, logf)
        r1, r2, r3, r4 = (b[j * B - 1:j * B, :] for j in (1, 2, 3, 4))
        start = jnp.where(row < B, 0.0, jnp.where(row < 2 * B, r1, jnp.where(row < 3 * B, r2, r3)))
        mid = jnp.where(row < 2 * B, r1, r3)
        bd = b - start
        ops = dict(
            qd=qq * jnp.exp(bd), kd=kk * jnp.exp(-bd),
            qm=qq * jnp.exp(jnp.minimum(b - mid, 0.0)), km=kk * jnp.exp(jnp.minimum(mid - b, 0.0)),
            qf=qq * jnp.exp(jnp.minimum(b - r2, 0.0)), kf=kk * jnp.exp(jnp.minimum(r2 - b, 0.0)),
            qs=qq * jnp.exp(b), ks=kk * jnp.exp(r4 - b))
        ops = {k: v.astype(BF16) for k, v in ops.items()}
        e4 = jnp.exp(r4)
        heads = [slice(h * HG_DK, (h + 1) * HG_DK) for h in range(HG_HEADS)]
        vs = [i_ref[0, r0:r0 + L, hl].astype(BF16) for hl in heads]
        a_d = [_dot_nt(ops['qd'][:, hl], ops['kd'][:, hl]) for hl in heads]
        a_m = [_dot_nt(ops['qm'][:, hl], ops['km'][:, hl]) for hl in heads]
        a_f = [_dot_nt(ops['qf'][:, hl], ops['kf'][:, hl]) for hl in heads]
        a = [(jnp.where(m_diag, a_d[h], 0.0) + jnp.where(m_mid, a_m[h], 0.0)
              + jnp.where(m_far, a_f[h], 0.0)).astype(BF16) for h in range(HG_HEADS)]
        s_old = [st_ref[h] for h in range(HG_HEADS)]
        o = [_dot(a[h], vs[h]) + _dot_nt(ops['qs'][:, heads[h]], s_old[h].astype(BF16)) for h in range(HG_HEADS)]
        for h, hl in enumerate(heads):
            st_ref[h] = e4[:, hl] * s_old[h] + _dot_tn(vs[h], ops['ks'][:, hl])
        for h, hl in enumerate(heads):
            rms = lax.rsqrt(jnp.mean(o[h] * o[h], -1, keepdims=True) + 1e-6)
            gv = g_ref[0, r0:r0 + L, hl]
            y_ref[0, r0:r0 + L, hl] = (o[h] * rms * gn_ref[:, hl] * (gv * _sigmoid(gv))).astype(y_ref.dtype)

    @pl.when(pl.program_id(1) == pl.num_programs(1) - 1)
    def _():
        for h in range(HG_HEADS):
            s_ref[0, h] = st_ref[h].T


def _hgrn(pa3, lb, gn, s0):
    b, t, _ = pa3.shape
    tt = min(t, 256)
    nt = t // tt
    seq = lambda blk: pl.BlockSpec((1, tt, BR_W), lambda i, j: (i, j, blk))
    vec = pl.BlockSpec((1, BR_W), lambda i, j: (0, 0))
    s_spec = pl.BlockSpec((1, HG_HEADS, HG_DK, HG_DK), lambda i, j: (i, 0, 0, 0))
    return pl.pallas_call(
        _hgrn_kernel,
        grid=(b, nt),
        in_specs=[seq(4), seq(5), seq(6), seq(7), vec, vec, s_spec],
        out_specs=[pl.BlockSpec((1, tt, BR_W), lambda i, j: (i, j, 0)), s_spec],
        out_shape=[jax.ShapeDtypeStruct((b, t, BR_W), BF16), jax.ShapeDtypeStruct(s0.shape, F32)],
        scratch_shapes=[pltpu.VMEM((HG_HEADS, HG_DK, HG_DK), F32)],
        compiler_params=_cparams("arbitrary", "arbitrary"),
        name="hgrn",
    )(pa3, pa3, pa3, pa3, lb, gn, s0)


RG_PAD = 8


def _rglru_tile(x, g, cw_ref, cb_ref, wa_ref, ba_ref, wx_ref, bx_ref, lam_ref,
                h0_ref, buf0_ref, y_ref, h_ref, buf_ref, xp_ref):
    tt = x.shape[0]
    hist = RG_CONV - 1

    @pl.when(pl.program_id(1) == 0)
    def _():
        h_ref[...] = h0_ref[...]
        xp_ref[RG_PAD - hist:RG_PAD, :] = buf0_ref[0]

    xp_ref[RG_PAD:RG_PAD + tt, :] = x
    u = cb_ref[...] + x * cw_ref[hist:hist + 1, :]
    for j in range(hist):
        u = u + xp_ref[RG_PAD - hist + j:RG_PAD - hist + j + tt, :] * cw_ref[j:j + 1, :]
    new_buf = xp_ref[RG_PAD + tt - hist:RG_PAD + tt, :]
    xp_ref[RG_PAD - hist:RG_PAD, :] = new_buf
    buf_ref[0] = new_buf

    ub = u.astype(BF16)
    r = _sigmoid(_dot(ub, wa_ref[...]) + ba_ref[...])
    ig = _sigmoid(_dot(ub, wx_ref[...]) + bx_ref[...])
    log_a = -RG_C * r * jax.nn.softplus(-lam_ref[...])
    a = jnp.exp(log_a)
    bt = jnp.sqrt(jnp.maximum(-jnp.tanh(log_a) * (a * a + 1.0), 0.0)) * (ig * u)
    n_grp = tt // SUBLANE
    a3 = a.reshape(n_grp, SUBLANE, RG_W)
    b3 = bt.reshape(n_grp, SUBLANE, RG_W)
    sub = lax.broadcasted_iota(jnp.int32, (1, SUBLANE, 1), 1)
    s = 1
    while s < SUBLANE:
        keep = sub >= s
        a_sh = jnp.where(keep, pltpu.roll(a3, s, 1), 1.0)
        b_sh = jnp.where(keep, pltpu.roll(b3, s, 1), 0.0)
        b3 = a3 * b_sh + b3
        a3 = a3 * a_sh
        s *= 2
    h_in = h_ref[0]
    hs = []
    for gi in range(n_grp):
        hs.append(a3[gi] * h_in + b3[gi])
        h_in = hs[-1][SUBLANE - 1:SUBLANE, :]
    h_ref[0] = h_in
    y_ref[0] = (jnp.concatenate(hs, axis=0) * _gelu(g)).astype(y_ref.dtype)


def _merge_kernel(x_ref, y0_ref, y1_ref, y2_ref, y3_ref, wmg_ref, bmg_ref, wbr_ref, wout_ref,
                  g_ref, b_ref, o_ref, *, alpha):
    x = x_ref[...]
    xb = x.astype(BF16)
    mix = None
    for m, y_ref in enumerate((y0_ref, y1_ref, y2_ref, y3_ref)):
        gate = _sigmoid(_dot(xb, wmg_ref[m]) + bmg_ref[m])
        term = gate * _dot(y_ref[...], wbr_ref[m])
        mix = term if mix is None else mix + term
    o_ref[...] = _layer_norm(alpha * x + _dot(mix.astype(BF16), wout_ref[...]), g_ref[...], b_ref[...])


def _merge(x2d, ys, wmg, bmg, wbr, wout, g, bb, alpha):
    m = x2d.shape[0]
    tm = min(m, DENSE_ROWS)
    row = lambda width: pl.BlockSpec((tm, width), lambda i: (i, 0))
    c2 = lambda shape: pl.BlockSpec(shape, lambda i: (0, 0))
    c3 = lambda shape: pl.BlockSpec(shape, lambda i: (0, 0, 0))
    return pl.pallas_call(
        functools.partial(_merge_kernel, alpha=alpha),
        grid=(m // tm,),
        in_specs=[row(D_MODEL)] + [row(BR_W)] * N_BRANCH +
                 [c3((N_BRANCH, D_MODEL, D_MODEL)), c3((N_BRANCH, 1, D_MODEL)),
                  c3((N_BRANCH, BR_W, D_MODEL)), c2((D_MODEL, D_MODEL)),
                  c2((1, D_MODEL)), c2((1, D_MODEL))],
        out_specs=row(D_MODEL),
        out_shape=jax.ShapeDtypeStruct((m, D_MODEL), F32),
        compiler_params=_cparams("arbitrary"),
        name="merge",
    )(x2d, *ys, wmg, bmg, wbr, wout, g, bb)


FF_PAD = 8
FF_COL_BLK = 512


def _ffn_kernel(x_ref, wg_ref, wu_ref, cw_ref, cb_ref, wd_ref, g_ref, b_ref, buf0_ref,
                o_ref, buf_ref, gp_ref, *, alpha):
    tt = x_ref.shape[1]
    hist = FFN_CONV - 1

    @pl.when(pl.program_id(1) == 0)
    def _():
        gp_ref[FF_PAD - hist:FF_PAD, :] = buf0_ref[0]

    x = x_ref[0]
    xb = x.astype(BF16)
    chunks = [slice(c0, c0 + FF_COL_BLK) for c0 in range(0, D_FF, FF_COL_BLK)]

    def in_proj(cols):
        gp = _dot(xb, wg_ref[:, cols])
        gp_ref[FF_PAD:FF_PAD + tt, cols] = gp
        return gp, _dot(xb, wu_ref[:, cols])

    def out_proj(cols, gp, up):
        gc = cb_ref[:, cols] + gp * cw_ref[hist:hist + 1, cols]
        for j in range(hist):
            gc = gc + gp_ref[FF_PAD - hist + j:FF_PAD - hist + j + tt, cols] * cw_ref[j:j + 1, cols]
        return _dot((_gelu(gc) * up).astype(BF16), wd_ref[cols, :])

    acc = None
    pending = in_proj(chunks[0])
    for i, cols in enumerate(chunks):
        nxt = in_proj(chunks[i + 1]) if i + 1 < len(chunks) else None
        part = out_proj(cols, *pending)
        acc = part if acc is None else acc + part
        pending = nxt
    new_buf = gp_ref[FF_PAD + tt - hist:FF_PAD + tt, :]
    gp_ref[FF_PAD - hist:FF_PAD, :] = new_buf
    buf_ref[0] = new_buf
    o_ref[0] = _layer_norm(alpha * x + acc, g_ref[...], b_ref[...])


def _ffn(x3, wg, wu, cw, cb, wd, g, bb, buf0, alpha):
    b, t, _ = x3.shape
    tt = min(t, DENSE_ROWS)
    nt = t // tt
    c2 = lambda shape: pl.BlockSpec(shape, lambda i, j: (0, 0))
    buf_spec = pl.BlockSpec((1, FFN_CONV - 1, D_FF), lambda i, j: (i, 0, 0))
    seq = pl.BlockSpec((1, tt, D_MODEL), lambda i, j: (i, j, 0))
    return pl.pallas_call(
        functools.partial(_ffn_kernel, alpha=alpha),
        grid=(b, nt),
        in_specs=[seq, c2((D_MODEL, D_FF)), c2((D_MODEL, D_FF)), c2((FFN_CONV, D_FF)), c2((1, D_FF)),
                  c2((D_FF, D_MODEL)), c2((1, D_MODEL)), c2((1, D_MODEL)), buf_spec],
        out_specs=[seq, buf_spec],
        out_shape=[jax.ShapeDtypeStruct(x3.shape, F32), jax.ShapeDtypeStruct(buf0.shape, F32)],
        scratch_shapes=[pltpu.VMEM((FF_PAD + tt, D_FF), F32)],
        compiler_params=_cparams("arbitrary", "arbitrary"),
        name="ffn",
    )(x3, wg, wu, cw, cb, wd, g, bb, buf0)


def _prep_layer(l, w_in, b_in, ml_norm_g, hg_norm_g, lbs, rg_conv_w, rg_conv_b, rg_w_a, rg_b_a,
                rg_w_x, rg_b_x, rg_lambda, w_mg, b_mg, w_br, w_out, ln1_g, ln1_b,
                w_ff_gate, w_ff_up, ff_conv_w, ff_conv_b, w_ff_down, ln2_g, ln2_b):
    offs = [int(o) for o in np.concatenate([[0], np.cumsum(PROJ_SIZES)])]
    wl, bl = w_in[l].astype(BF16), b_in[l]
    cols = lambda a, segs: [a[..., offs[s]:offs[s + 1]] for s in segs]
    w_small = jnp.concatenate(cols(wl, SMALL_SEGS), axis=1)
    b_small = jnp.concatenate(cols(bl, SMALL_SEGS))
    w = jnp.concatenate(cols(wl, MAIN_SEGS) + [w_small, jnp.zeros((D_MODEL, SMALL_W - N_SMALL), BF16)]
                        + cols(wl, RG_SEGS) + cols(wl, KV_SEGS), axis=1)
    bvec = jnp.concatenate(cols(bl, MAIN_SEGS) + [b_small, jnp.zeros((SMALL_W - N_SMALL,), F32)]
                           + cols(bl, RG_SEGS) + cols(bl, KV_SEGS))[None, :]
    wst = w_small.T
    bst = b_small[:, None]

    def block_diag(wb):
        eye = jnp.eye(RG_BLOCKS, dtype=F32)
        return jnp.einsum('nde,nm->ndme', wb, eye).reshape(RG_W, RG_W).astype(BF16)

    row = lambda v: v[None, :].astype(F32)
    return dict(
        w=w, b=bvec, wst=wst, bst=bst, ml_g=row(ml_norm_g[l]), hg_g=row(hg_norm_g[l]), lb=row(lbs[l]),
        rg_cw=rg_conv_w[l], rg_cb=row(rg_conv_b[l]), rg_wa=block_diag(rg_w_a[l]), rg_ba=row(rg_b_a[l]),
        rg_wx=block_diag(rg_w_x[l]), rg_bx=row(rg_b_x[l]), rg_lam=row(rg_lambda[l]),
        wmg=w_mg[l].astype(BF16), bmg=b_mg[l][:, None, :], wbr=w_br[l].astype(BF16),
        wout=w_out[l].astype(BF16), ln1_g=row(ln1_g[l]), ln1_b=row(ln1_b[l]),
        wg=w_ff_gate[l].astype(BF16), wu=w_ff_up[l].astype(BF16), ff_cw=ff_conv_w[l],
        ff_cb=row(ff_conv_b[l]), wd=w_ff_down[l].astype(BF16), ln2_g=row(ln2_g[l]), ln2_b=row(ln2_b[l]))


def _kv_t_minor(t):
    return t >= LANE


def _trunk_layer(x, p, fox_past, ml_c, ml_n, ml_m, hg_s, rg_h, rg_buf, ff_buf, alpha, layer, depth, kv_bufs):
    b, t, _ = x.shape
    assert t % 64 == 0 and (t <= 256 or t % 256 == 0)
    x2d = x.reshape(b * t, D_MODEL)
    rg = (p['rg_cw'], p['rg_cb'], p['rg_wa'], p['rg_ba'], p['rg_wx'], p['rg_bx'], p['rg_lam'],
          rg_h.reshape(b, 1, RG_W), rg_buf)
    pa3, k32, v32, kb, vt, st, y_rg, rg_h, rg_buf = _proj(x, p['w'], p['b'], p['wst'], p['bst'], rg, layer,
                                                          depth, _kv_t_minor(t), kv_bufs)
    y_ml, ml_c, ml_n, ml_m = _mlstm(pa3, st, ml_c, ml_n, ml_m.reshape(b, 1, ML_HEADS), p['ml_g'])
    past = None
    if fox_past is not None:
        k_cache, v_cache, logf_past = fox_past
        to_feature_major = lambda a: jnp.transpose(a, (0, 1, 3, 4, 2)).reshape(depth, b, BR_W, a.shape[2])
        past = (to_feature_major(k_cache), to_feature_major(v_cache), logf_past)
    y_fx, f_log = _fox(pa3, st, kb, vt, past, layer)
    f_log = jnp.swapaxes(f_log, 1, 2)
    y_hg, hg_s = _hgrn(pa3, p['lb'], p['hg_g'], hg_s)
    ys = [y.reshape(b * t, BR_W) for y in (y_ml, y_fx, y_hg, y_rg)]
    x1 = _merge(x2d, ys, p['wmg'], p['bmg'], p['wbr'], p['wout'], p['ln1_g'], p['ln1_b'], alpha)
    x2, ff_buf = _ffn(x1.reshape(b, t, D_MODEL), p['wg'], p['wu'], p['ff_cw'], p['ff_cb'], p['wd'],
                      p['ln2_g'], p['ln2_b'], ff_buf, alpha)
    state = (f_log, ml_c, ml_n, ml_m.reshape(b, ML_HEADS), hg_s, rg_h.reshape(b, RG_W), rg_buf, ff_buf)
    return x2, (k32, v32), state


def _kv_output(buf, t):
    depth, b = buf.shape[:2]
    if _kv_t_minor(t):
        return jnp.transpose(buf.reshape(depth, b, FX_HEADS, FX_HD, t), (0, 1, 4, 2, 3))
    return buf.reshape(depth, b, t, FX_HEADS, FX_HD)


def kernel(x_prompt, x_sample, cache_fox_k, cache_fox_v, cache_fox_logf, state_mlstm_c, state_mlstm_n,
           state_mlstm_m, state_hgrn_s, state_rglru_h, state_rglru_conv, state_ffn_conv,
           w_in, b_in, ml_norm_g, hg_norm_g, hg_lb_logits, rg_conv_w, rg_conv_b, rg_w_a, rg_b_a,
           rg_w_x, rg_b_x, rg_lambda, w_mg, b_mg, w_br, w_out, ln1_g, ln1_b,
           w_ff_gate, w_ff_up, ff_conv_w, ff_conv_b, w_ff_down, ln2_g, ln2_b):
    depth = w_in.shape[0]
    alpha = (2 * depth) ** 0.25
    pl_soft = jax.nn.softmax(hg_lb_logits.astype(F32), axis=0)
    lbs = jnp.cumsum(pl_soft, axis=0) - pl_soft[0]
    bp = x_prompt.shape[0]
    yp, ys = x_prompt, x_sample
    p_new, s_new = [], []
    kv_p, kv_s = (), ()
    for l in range(depth):
        p = _prep_layer(l, w_in, b_in, ml_norm_g, hg_norm_g, lbs, rg_conv_w, rg_conv_b, rg_w_a, rg_b_a,
                        rg_w_x, rg_b_x, rg_lambda, w_mg, b_mg, w_br, w_out, ln1_g, ln1_b,
                        w_ff_gate, w_ff_up, ff_conv_w, ff_conv_b, w_ff_down, ln2_g, ln2_b)
        z = lambda *shape: jnp.zeros((bp,) + shape, F32)
        yp, kv_p, st_p = _trunk_layer(yp, p, None, z(ML_HEADS, ML_DK, ML_DV), z(ML_HEADS, ML_DK), z(ML_HEADS),
                                      z(HG_HEADS, HG_DK, HG_DK), z(RG_W), z(RG_CONV - 1, RG_W),
                                      z(FFN_CONV - 1, D_FF), alpha, l, depth, kv_p if l else ())
        p_new.append(st_p)
        ys, kv_s, st_s = _trunk_layer(ys, p, (cache_fox_k, cache_fox_v, cache_fox_logf[l]),
                                      state_mlstm_c[l], state_mlstm_n[l], state_mlstm_m[l], state_hgrn_s[l],
                                      state_rglru_h[l], state_rglru_conv[l], state_ffn_conv[l], alpha,
                                      l, depth, kv_s if l else ())
        s_new.append(st_s)
    n_st = len(p_new[0])
    p_out = [jnp.stack([st[j] for st in p_new]) for j in range(n_st)]
    s_out = [jnp.stack([st[j] for st in s_new]) for j in range(n_st)]
    tp, ts = x_prompt.shape[1], x_sample.shape[1]
    return (yp, ys, _kv_output(kv_p[0], tp), _kv_output(kv_p[1], tp), *p_out,
            _kv_output(kv_s[0], ts), _kv_output(kv_s[1], ts), *s_out)
```

```python
import functools

import numpy as np
import jax
import jax.numpy as jnp
from jax import lax
from jax.experimental import pallas as pl
from jax.experimental.pallas import tpu as pltpu

F32 = jnp.float32
BF16 = jnp.bfloat16

D_MODEL = 1024
BR_W = D_MODEL // 2
ML_HEADS = 4
ML_DV = BR_W // ML_HEADS
ML_DK = ML_DV // 2
ML_CHUNK = 128
FX_HD = 64
FX_HEADS = BR_W // FX_HD
FX_PAST_TILE = 128
FX_KEY_TILE = 256
LOG2E = 1.4426950408889634
HG_HEADS = 4
HG_DK = BR_W // HG_HEADS
HG_BLOCK = 16
HG_CHUNK = 64
RG_W = BR_W
RG_BLOCKS = 8
RG_BD = RG_W // RG_BLOCKS
RG_CONV = 4
RG_C = 8.0
D_FF = 2 * D_MODEL
FFN_CONV = 3
N_BRANCH = 4
PROJ_SIZES = (ML_HEADS * ML_DK, ML_HEADS * ML_DK, BR_W, BR_W, ML_HEADS, ML_HEADS,
              BR_W, BR_W, BR_W, FX_HEADS,
              BR_W, BR_W, BR_W, BR_W,
              RG_W, RG_W)
NEG_BIG = -1e30
TINY = 1e-30

LANE = 128
SUBLANE = 8
SMALL_W = LANE
N_SMALL = 2 * ML_HEADS + FX_HEADS
MAIN_SEGS = (6,)
SMALL_SEGS = (4, 5, 9)
STASH_SEGS = (10, 11, 12, 13, 14, 15, 0, 1, 2, 3)
KV_SEGS = (7, 8)
N_MAIN = sum(PROJ_SIZES[i] for i in MAIN_SEGS)
PA_W = N_MAIN + SMALL_W
STASH_COL0 = PA_W
KV_COL0 = STASH_COL0 + sum(PROJ_SIZES[i] for i in STASH_SEGS)
COL_BLK = 512
DENSE_ROWS = 512
VMEM_LIMIT = 56 * 1024 * 1024


def _cparams(*sem):
    return pltpu.CompilerParams(dimension_semantics=sem, vmem_limit_bytes=VMEM_LIMIT)


def _dot(a, b):
    return jnp.dot(a, b, preferred_element_type=F32)


def _dot_nt(a, b):
    return lax.dot_general(a, b, (((1,), (1,)), ((), ())), preferred_element_type=F32)


def _dot_tn(a, b):
    return lax.dot_general(a, b, (((0,), (0,)), ((), ())), preferred_element_type=F32)


def _split3(x):
    hi = x.astype(BF16)
    r1 = x - hi.astype(F32)
    mid = r1.astype(BF16)
    lo = (r1 - mid.astype(F32)).astype(BF16)
    return hi, mid, lo


def _tri_dot(tri, x):
    t = tri.astype(BF16)
    hi, mid, lo = _split3(x)
    return (_dot(t, lo) + _dot(t, mid)) + _dot(t, hi)


def _dot_tri(x, tri):
    t = tri.astype(BF16)
    hi, mid, lo = _split3(x)
    return (_dot(lo, t) + _dot(mid, t)) + _dot(hi, t)


def _tri(n, lower, block=None):
    r = lax.broadcasted_iota(jnp.int32, (n, n), 0)
    c = lax.broadcasted_iota(jnp.int32, (n, n), 1)
    m = (r >= c) if lower else (r <= c)
    if block is not None:
        m = m & ((r // block) == (c // block))
    return m.astype(F32)


def _dot_hi_nt(a, b):
    return lax.dot_general(a, b, (((1,), (1,)), ((), ())), precision=lax.Precision.HIGHEST,
                           preferred_element_type=F32)


_log_sigmoid = jax.nn.log_sigmoid
_sigmoid = jax.nn.sigmoid
_gelu = jax.nn.gelu


def _layer_norm(v, g, b):
    mu = jnp.mean(v, -1, keepdims=True)
    c = v - mu
    var = jnp.mean(c * c, -1, keepdims=True)
    return c * lax.rsqrt(var + 1e-5) * g + b


N_RG_IN = 9
N_HG_IN = 3
N_ML_IN = 4
N_PROJ_IN = 5 + N_HG_IN + N_RG_IN + N_ML_IN
STASH_W = 9 * COL_BLK + SMALL_W


def _proj_kernel(*refs, t_minor, n_alias, nt):
    x_ref, w_ref, b_ref, wst_ref, bst_ref = refs[:5]
    hg_in = refs[5:5 + N_HG_IN]
    rg_in = refs[5 + N_HG_IN:5 + N_HG_IN + N_RG_IN]
    ml_in = refs[5 + N_HG_IN + N_RG_IN:N_PROJ_IN]
    (pa_ref, k32_ref, v32_ref, kb_ref, vb_ref, st_ref,
     yhg_ref, hgs_ref, yrg_ref, rgh_ref, rgbuf_ref, yml_ref, mlc_ref, mln_ref, mlm_ref,
     hgst_ref, xp_ref, mlct_ref, stash_ref, stash_t_ref) = refs[N_PROJ_IN + n_alias:]
    s = pl.program_id(0)
    slot_w = lax.rem(s, 2)
    slot_r = 1 - slot_w
    t_prev = lax.rem(jnp.maximum(s - 1, 0), nt)
    first, last = t_prev == 0, t_prev == nt - 1

    @pl.when(s == 0)
    def _():
        stash_ref[1] = jnp.zeros(stash_ref.shape[1:], F32)
        stash_t_ref[1] = jnp.zeros(stash_t_ref.shape[1:], F32)

    xb = x_ref[0].astype(BF16)
    col = lambda c0, width: _dot(xb, w_ref[:, c0:c0 + width]) + b_ref[:, c0:c0 + width]
    stashed = lambda c0, width: stash_ref.at[slot_r, :, c0:c0 + width]
    blk = lambda j: stashed(j * COL_BLK, COL_BLK)
    hg_steps = _hgrn_tile(blk(0), blk(1), blk(2), blk(3), first, last, *hg_in, yhg_ref, hgs_ref, hgst_ref)
    rg_steps = _rglru_tile(blk(4), blk(5), first, *rg_in, yrg_ref, rgh_ref, rgbuf_ref, xp_ref)
    qk0 = 6 * COL_BLK
    ml_steps = _mlstm_tile(stashed(qk0, ML_HEADS * ML_DK), stashed(qk0 + ML_HEADS * ML_DK, ML_HEADS * ML_DK),
                           blk(7), blk(8), stashed(9 * COL_BLK, SMALL_W), stash_t_ref.at[slot_r],
                           first, last, *ml_in, yml_ref, mlc_ref, mln_ref, mlm_ref, mlct_ref)

    def store_stash(j):
        stash_ref[slot_w, :, j * COL_BLK:(j + 1) * COL_BLK] = col(STASH_COL0 + j * COL_BLK, COL_BLK)

    def store_main(c0):
        width = min(COL_BLK, PA_W - c0)
        val = col(c0, width)
        pa_ref[0, :, c0:c0 + width] = val
        if c0 == N_MAIN:
            stash_ref[slot_w, :, 9 * COL_BLK:] = val

    def store_rows():
        st = _dot_nt(wst_ref[...], xb) + bst_ref[...]
        st_ref[0] = st
        stash_t_ref[slot_w] = st

    def store_k():
        k = col(KV_COL0, BR_W)
        kb_ref[0] = k.astype(BF16)
        k32_ref[0, 0] = k.T if t_minor else k

    def store_v():
        v = col(KV_COL0 + BR_W, BR_W)
        vt = v.T
        vb_ref[0] = vt.astype(BF16)
        v32_ref[0, 0] = vt if t_minor else v

    proj_steps = ([functools.partial(store_main, c0) for c0 in range(0, PA_W, COL_BLK)]
                  + [store_rows, store_k, store_v] + [functools.partial(store_stash, j) for j in range(9)])
    tagged = [((i + 0.5) / len(steps), steps[i]) for steps in (hg_steps, ml_steps, rg_steps, proj_steps)
              for i in range(len(steps))]
    for _, step in sorted(tagged, key=lambda ps: ps[0]):
        step()


def _proj(x3, w, b, wst, bst, hg, rg, ml, layer, depth, t_minor, kv_bufs):
    bsz, t, _ = x3.shape
    tt = min(t, 256)
    nt = t // tt
    n_tiles = bsz * nt
    nw = w.shape[1]
    cur = lambda s: divmod(jnp.minimum(s, n_tiles - 1), nt)
    prev = lambda s: divmod(jnp.maximum(s - 1, 0), nt)
    seq = lambda width: pl.BlockSpec((1, tt, width), lambda s: (*cur(s), 0))
    seq_prev = lambda width: pl.BlockSpec((1, tt, width), lambda s: (*prev(s), 0))
    const = lambda shape: pl.BlockSpec(shape, lambda s: (0, 0))
    sds = lambda width, dt: jax.ShapeDtypeStruct((bsz, t, width), dt)
    vec = const((1, RG_W))
    mat = const((RG_W, RG_W))
    h_spec = pl.BlockSpec((1, 1, RG_W), lambda s: (prev(s)[0], 0, 0))
    buf_spec = pl.BlockSpec((1, RG_CONV - 1, RG_W), lambda s: (prev(s)[0], 0, 0))
    rg_specs = [const((RG_CONV, RG_W)), vec, mat, vec, mat, vec, vec, h_spec, buf_spec]
    s_spec = pl.BlockSpec((1, HG_HEADS, HG_DK, HG_DK), lambda s: (prev(s)[0], 0, 0, 0))
    hg_specs = [vec, vec, s_spec]
    c_spec = pl.BlockSpec((1, ML_HEADS, ML_DK, ML_DV), lambda s: (prev(s)[0], 0, 0, 0))
    n_spec = pl.BlockSpec((1, ML_HEADS, ML_DK), lambda s: (prev(s)[0], 0, 0))
    m_spec = pl.BlockSpec((1, 1, ML_HEADS), lambda s: (prev(s)[0], 0, 0))
    ml_specs = [c_spec, n_spec, m_spec, vec]
    assert len(rg) == N_RG_IN and len(hg) == N_HG_IN and len(ml) == N_ML_IN
    if t_minor:
        kv_spec = pl.BlockSpec((1, 1, BR_W, tt), lambda s: (layer, cur(s)[0], 0, cur(s)[1]))
        kv_sds = jax.ShapeDtypeStruct((depth, bsz, BR_W, t), F32)
    else:
        kv_spec = pl.BlockSpec((1, 1, tt, BR_W), lambda s: (layer, *cur(s), 0))
        kv_sds = jax.ShapeDtypeStruct((depth, bsz, t, BR_W), F32)
    n_alias = len(kv_bufs)
    return pl.pallas_call(
        functools.partial(_proj_kernel, t_minor=t_minor, n_alias=n_alias, nt=nt),
        grid=(n_tiles + 1,),
        in_specs=[seq(D_MODEL), const((D_MODEL, nw)), const((1, nw)),
                  const((2 * 8, D_MODEL)), const((2 * 8, 1))] + hg_specs + rg_specs + ml_specs
                 + [pl.BlockSpec(memory_space=pl.ANY)] * n_alias,
        out_specs=[seq(PA_W), kv_spec, kv_spec, seq(BR_W),
                   pl.BlockSpec((1, BR_W, tt), lambda s: (cur(s)[0], 0, cur(s)[1])),
                   pl.BlockSpec((1, 2 * 8, tt), lambda s: (cur(s)[0], 0, cur(s)[1])),
                   seq_prev(BR_W), s_spec, seq_prev(RG_W), h_spec, buf_spec,
                   seq_prev(BR_W), c_spec, n_spec, m_spec],
        out_shape=[sds(PA_W, F32), kv_sds, kv_sds, sds(BR_W, BF16),
                   jax.ShapeDtypeStruct((bsz, BR_W, t), BF16),
                   jax.ShapeDtypeStruct((bsz, 2 * 8, t), F32),
                   sds(BR_W, BF16), jax.ShapeDtypeStruct((bsz, HG_HEADS, HG_DK, HG_DK), F32),
                   sds(RG_W, BF16), jax.ShapeDtypeStruct((bsz, 1, RG_W), F32),
                   jax.ShapeDtypeStruct((bsz, RG_CONV - 1, RG_W), F32),
                   sds(BR_W, BF16), jax.ShapeDtypeStruct((bsz, ML_HEADS, ML_DK, ML_DV), F32),
                   jax.ShapeDtypeStruct((bsz, ML_HEADS, ML_DK), F32),
                   jax.ShapeDtypeStruct((bsz, 1, ML_HEADS), F32)],
        input_output_aliases={N_PROJ_IN + a: 1 + a for a in range(n_alias)},
        scratch_shapes=[pltpu.VMEM((HG_HEADS, HG_DK, HG_DK), F32), pltpu.VMEM((RG_PAD + tt, RG_W), F32),
                        pltpu.VMEM((ML_HEADS, ML_DV, ML_DK), F32),
                        pltpu.VMEM((2, tt, STASH_W), F32), pltpu.VMEM((2, 2 * 8, tt), F32)],
        compiler_params=_cparams("arbitrary"),
        name="proj",
    )(x3, w, b, wst, bst, *hg, *rg, *ml, *kv_bufs)


def _mlstm_tile(q_ref, k_ref, v_ref, o_ref, sm_ref, st_ref, first, last, c0_ref, n0_ref, m0_ref, g_ref,
                y_ref, c_ref, n_ref, m_ref, ct_ref):
    tt = q_ref.shape[0]
    L = min(tt, ML_CHUNK)
    H = range(ML_HEADS)

    @pl.when(first)
    def _():
        for h in H:
            ct_ref[h] = c0_ref[0, h].T
        n_ref[...] = n0_ref[...]
        m_ref[...] = m0_ref[...]

    tri_l = _tri(L, True)
    tri_u = _tri(L, False)
    causal_t = tri_u > 0.5
    dk = [slice(h * ML_DK, (h + 1) * ML_DK) for h in H]
    dv = [slice(h * ML_DV, (h + 1) * ML_DV) for h in H]
    n_chunk = tt // L
    ctx = [dict() for _ in range(n_chunk)]

    def gates(c):
        k = ctx[c]
        rows = slice(c * L, (c + 1) * L)
        sm = sm_ref[rows, :]
        st = st_ref[:, rows]
        k['i_col'], k['lf_col'] = sm[:, 0:ML_HEADS], _log_sigmoid(sm[:, ML_HEADS:2 * ML_HEADS])
        k['lf_row'] = _log_sigmoid(st[ML_HEADS:2 * ML_HEADS, :])
        k['qf'] = [q_ref[rows, dk[h]] for h in H]
        k['qb'] = [x.astype(BF16) for x in k['qf']]
        k['kf'] = [k_ref[rows, dk[h]] * (ML_DK ** -0.5) for h in H]
        k['vb'] = [v_ref[rows, dv[h]].astype(BF16) for h in H]

    def cumsums(c):
        k = ctx[c]
        k['a_cols'] = k['i_col'] - _tri_dot(tri_l, k['lf_col'])
        k['fcum_row'] = _dot_tri(k['lf_row'], tri_u)
        k['s_t'] = [_dot_nt(k['kf'][h].astype(BF16), k['qb'][h]) for h in H]

    def weights(c):
        k = ctx[c]
        m_all = m_ref[0]
        k['w_t'], k['g'], k['m_t'], k['ks'] = [], [], [], []
        for h in H:
            fr = k['fcum_row'][h:h + 1, :]
            a_col = k['a_cols'][:, h:h + 1]
            d_t = jnp.where(causal_t, fr + a_col, NEG_BIG)
            prev = fr + m_all[:, h:h + 1]
            m_t = jnp.maximum(prev, jnp.max(d_t, 0, keepdims=True))
            k['m_t'].append(m_t)
            k['w_t'].append(jnp.exp(d_t - m_t) * k['s_t'][h])
            k['g'].append(jnp.exp(prev - m_t))
            k['ks'].append(jnp.exp(fr[:, L - 1:L] + a_col - m_t[:, L - 1:L]) * k['kf'][h])
        head = lax.broadcasted_iota(jnp.int32, (1, ML_HEADS), 1)
        m_vec = k['m_t'][0][:, L - 1:L]
        for h in range(1, ML_HEADS):
            m_vec = jnp.where(head == h, k['m_t'][h][:, L - 1:L], m_vec)
        m_ref[0] = m_vec

    def products(c):
        k = ctx[c]
        k['ct'] = [ct_ref[h] for h in H]
        k['nst'] = [n_ref[0, h:h + 1, :] for h in H]
        k['num_t'] = [_dot_tn(k['vb'][h], k['w_t'][h].astype(BF16)) for h in H]
        k['qc_t'] = [_dot_nt(k['ct'][h].astype(BF16), k['qb'][h]) for h in H]
        k['qn'] = [_dot_hi_nt(k['nst'][h], k['qf'][h]) for h in H]
        k['dct'] = [_dot_tn(k['vb'][h], k['ks'][h].astype(BF16)) for h in H]

    def finish_chunk(c):
        k = ctx[c]
        rows = slice(c * L, (c + 1) * L)
        for h in H:
            g_last = k['g'][h][:, L - 1:L]
            ct_ref[h] = g_last * k['ct'][h] + k['dct'][h]
            n_ref[0, h:h + 1, :] = g_last * k['nst'][h] + jnp.sum(k['ks'][h], 0, keepdims=True)
        for h in H:
            g, m_t = k['g'][h], k['m_t'][h]
            den = jnp.sum(k['w_t'][h], 0, keepdims=True) + g * k['qn'][h]
            h_t = (k['num_t'][h] + g * k['qc_t'][h]) / jnp.maximum(jnp.abs(den), jnp.exp(-m_t))
            mu = jnp.mean(h_t, 0, keepdims=True)
            hc = h_t - mu
            var = jnp.mean(hc * hc, 0, keepdims=True)
            hn = (hc * lax.rsqrt(var + 1e-5)).T
            yv = _sigmoid(o_ref[rows, dv[h]]) * (hn * g_ref[:, dv[h]])
            y_ref[0, rows, dv[h]] = yv.astype(y_ref.dtype)

    def finish():
        @pl.when(last)
        def _():
            for h in H:
                c_ref[0, h] = ct_ref[h].T

    stages = (gates, cumsums, weights, products, finish_chunk)
    steps = []
    skew = 2
    for slot in range(len(stages) + skew * (n_chunk - 1)):
        for c in range(n_chunk):
            if 0 <= slot - skew * c < len(stages):
                steps.append(functools.partial(stages[slot - skew * c], c))
    return steps + [finish]


def _fox_kernel(*refs, n_past, tk):
    if n_past:
        (q_ref, sm_ref, st_ref, kb_ref, vt_ref, kp_ref, vp_ref, lfp_ref,
         y_ref, lf_ref, ccol_ref, crun_ref, rrun_ref, yt_ref, qm_ref, m_ref, l_ref) = refs
    else:
        (q_ref, sm_ref, st_ref, kb_ref, vt_ref,
         y_ref, lf_ref, ccol_ref, crun_ref, rrun_ref, yt_ref, qm_ref, m_ref, l_ref) = refs
    tq = q_ref.shape[1]
    tkp = FX_PAST_TILE
    single = kb_ref.shape[1] == tq
    t_idx = 0 if single else pl.program_id(1)
    tri_l = _tri(tq, True)
    tri_u = _tri(tq, False)

    def init():
        crun_ref[...] = jnp.zeros_like(crun_ref)
        if n_past:
            tri_p = _tri(tkp, True)
            for j in range(n_past // tkp):
                cc_p = crun_ref[...] + _tri_dot(tri_p, lfp_ref[0, j * tkp:(j + 1) * tkp, :])
                ccol_ref[j * tkp:(j + 1) * tkp, :] = cc_p * LOG2E
                crun_ref[...] = cc_p[tkp - 1:tkp, :]
        eye = (lax.broadcasted_iota(jnp.int32, (FX_HEADS, FX_HEADS), 0)
               == lax.broadcasted_iota(jnp.int32, (FX_HEADS, FX_HEADS), 1))
        rrun_ref[...] = jnp.sum(jnp.where(eye, crun_ref[...], 0.0), -1, keepdims=True)

    if single:
        init()
    else:
        pl.when(t_idx == 0)(init)

    lf_col = _log_sigmoid(sm_ref[0, :, 2 * ML_HEADS:N_SMALL])
    lf_row = _log_sigmoid(st_ref[0, 2 * ML_HEADS:N_SMALL, :])
    lf_ref[0] = lf_row
    cc = crun_ref[...] + _tri_dot(tri_l, lf_col)
    cr = rrun_ref[...] + _dot_tri(lf_row, tri_u)
    crun_ref[...] = cc[tq - 1:tq, :]
    rrun_ref[...] = cr[:, tq - 1:tq]
    new0 = n_past if single else pl.multiple_of(n_past + t_idx * tq, tq)
    ccol_ref[pl.ds(new0, tq), :] = cc * LOG2E
    cq_rows = cr * LOG2E

    n_sub = tq // tk
    krow = lax.broadcasted_iota(jnp.int32, (tk, tq), 0)
    qcol = lax.broadcasted_iota(jnp.int32, (tk, tq), 1)
    first_head = lax.broadcasted_iota(jnp.int32, (2 * FX_HD, 1), 0) < FX_HD
    qt = (q_ref[0] * (FX_HD ** -0.5 * LOG2E)).T

    for h in range(FX_HEADS):
        qp = qt[(h // 2) * 2 * FX_HD:(h // 2 + 1) * 2 * FX_HD]
        qm_ref[h] = (jnp.where(first_head, qp, 0.0) if h % 2 == 0 else jnp.where(first_head, 0.0, qp)).astype(BF16)
    yt_ref[...] = jnp.zeros_like(yt_ref)
    m_ref[...] = jnp.full(m_ref.shape, NEG_BIG, F32)
    l_ref[...] = jnp.zeros_like(l_ref)

    def tile(k_pairs, vt_pairs, ck, mask, k_is_transposed=False):
        qk = _dot_tn if k_is_transposed else _dot
        ss = [qk(k_pairs[h // 2], qm_ref[h]) for h in range(FX_HEADS)]
        ps, alphas = [], []
        for h in range(FX_HEADS):
            m_, l_ = m_ref[h:h + 1, :], l_ref[h:h + 1, :]
            u = ss[h] - ck[:, h:h + 1]
            if mask is not None:
                u = jnp.where(mask, u, NEG_BIG)
            cq = cq_rows[h:h + 1, :]
            mn = jnp.maximum(m_, jnp.max(u, 0, keepdims=True) + cq)
            pe = jnp.exp2(u + (cq - mn))
            a = jnp.exp2(m_ - mn)
            m_ref[h:h + 1, :] = mn
            l_ref[h:h + 1, :] = a * l_ + jnp.sum(pe, 0, keepdims=True)
            ps.append(pe.astype(BF16))
            alphas.append(a)
        for h in range(FX_HEADS):
            p, e = divmod(h, 2)
            hrows = slice(h * FX_HD, (h + 1) * FX_HD)
            yt_ref[hrows, :] = alphas[h] * yt_ref[hrows, :] + _dot(vt_pairs[p][e * FX_HD:(e + 1) * FX_HD, :], ps[h])

    pair_lanes = [slice(p * 2 * FX_HD, (p + 1) * 2 * FX_HD) for p in range(FX_HEADS // 2)]
    if n_past:
        def past_body(j, _):
            k0 = pl.multiple_of(j * tkp, tkp)
            ks = [kp_ref[0, 0, pp, pl.ds(k0, tkp)].astype(BF16) for pp in pair_lanes]
            vs = [vp_ref[0, 0, pp, pl.ds(k0, tkp)].astype(BF16) for pp in pair_lanes]
            tile(ks, vs, ccol_ref[pl.ds(k0, tkp), :], None, k_is_transposed=True)
            return 0
        lax.fori_loop(0, n_past // tkp, past_body, 0)

    if not single:
        def new_body(j, _):
            k0 = pl.multiple_of(j * tk, tk)
            ks = [kb_ref[0, pl.ds(k0, tk), pp] for pp in pair_lanes]
            vs = [vt_ref[0, pp, pl.ds(k0, tk)] for pp in pair_lanes]
            tile(ks, vs, ccol_ref[pl.ds(pl.multiple_of(n_past + j * tk, tk), tk), :], None)
            return 0
        lax.fori_loop(0, t_idx * n_sub, new_body, 0)

    for i in range(n_sub):
        k0 = i * tk if single else pl.multiple_of(t_idx * tq + i * tk, tk)
        ks = [kb_ref[0, pl.ds(k0, tk), pp] for pp in pair_lanes]
        vs = [vt_ref[0, pp, pl.ds(k0, tk)] for pp in pair_lanes]
        tile(ks, vs, ccol_ref[pl.ds(n_past + k0, tk), :], krow + i * tk <= qcol)
    for h in range(FX_HEADS):
        hrows = slice(h * FX_HD, (h + 1) * FX_HD)
        yt_ref[hrows, :] = yt_ref[hrows, :] / l_ref[h:h + 1, :]
    y_ref[0] = yt_ref[...].T.astype(y_ref.dtype)


def _fox(pa3, st, kb3, vt3, past=None, layer=0):
    b, t, _ = pa3.shape
    tq = min(t, 256)
    tk = min(tq, FX_KEY_TILE)
    nt = t // tq
    n_past = 0 if past is None else past[2].shape[1]
    seq = lambda width, blk: pl.BlockSpec((1, tq, width), lambda i, j: (i, j, blk))
    full = lambda n, width: pl.BlockSpec((1, n, width), lambda i, j: (i, 0, 0))
    in_specs = [seq(BR_W, 0), seq(SMALL_W, N_MAIN // SMALL_W),
                pl.BlockSpec((1, 2 * 8, tq), lambda i, j: (i, 0, j)),
                full(t, BR_W), full(BR_W, t)]
    args = [pa3, pa3, st, kb3, vt3]
    assert n_past % FX_PAST_TILE == 0
    if n_past:
        cache = pl.BlockSpec((1, 1, BR_W, n_past), lambda i, j: (layer, i, 0, 0))
        in_specs += [cache, cache, full(n_past, FX_HEADS)]
        args += list(past)
    return pl.pallas_call(
        functools.partial(_fox_kernel, n_past=n_past, tk=tk),
        grid=(b, nt),
        in_specs=in_specs,
        out_specs=[pl.BlockSpec((1, tq, BR_W), lambda i, j: (i, j, 0)),
                   pl.BlockSpec((1, FX_HEADS, tq), lambda i, j: (i, 0, j))],
        out_shape=[jax.ShapeDtypeStruct((b, t, BR_W), BF16),
                   jax.ShapeDtypeStruct((b, FX_HEADS, t), F32)],
        scratch_shapes=[pltpu.VMEM((n_past + t, FX_HEADS), F32),
                        pltpu.VMEM((1, FX_HEADS), F32), pltpu.VMEM((FX_HEADS, 1), F32),
                        pltpu.VMEM((BR_W, tq), F32), pltpu.VMEM((FX_HEADS, 2 * FX_HD, tq), BF16),
                        pltpu.VMEM((FX_HEADS, tq), F32), pltpu.VMEM((FX_HEADS, tq), F32)],
        compiler_params=_cparams("arbitrary", "arbitrary"),
        name="fox",
    )(*args)


def _hgrn_tile(zf, vi, qv_all, gv_all, first, last, lb_ref, gn_ref, s0_ref, y_ref, s_ref, st_ref):
    tt = zf.shape[0]
    L = HG_CHUNK
    B = HG_BLOCK
    assert L == 4 * B

    @pl.when(first)
    def _():
        for h in range(HG_HEADS):
            st_ref[h] = s0_ref[0, h].T

    tri = _tri(L, True)
    row = lax.broadcasted_iota(jnp.int32, (L, 1), 0)
    rb = lax.broadcasted_iota(jnp.int32, (L, L), 0) // B
    cb = lax.broadcasted_iota(jnp.int32, (L, L), 1) // B
    m_diag = _tri(L, True, B) > 0.5
    m_mid = ((rb == 1) & (cb == 0)) | ((rb == 3) & (cb == 2))
    m_far = (rb >= 2) & (cb < 2)
    lb = lb_ref[...]
    heads = [slice(h * HG_DK, (h + 1) * HG_DK) for h in range(HG_HEADS)]
    ctx = [dict() for _ in range(tt // L)]

    def gates(c):
        k = ctx[c]
        z = zf[c * L:(c + 1) * L, :]
        k['logf'] = jnp.log(jnp.maximum(lb + (1.0 - lb) * _sigmoid(z), TINY))
        k['kk'] = (1.0 - lb) * _sigmoid(-z)
        qv = qv_all[c * L:(c + 1) * L, :]
        k['qq'] = qv * _sigmoid(qv)

    def cumsum(c):
        ctx[c]['b'] = _tri_dot(tri, ctx[c]['logf'])

    def decays(c):
        k = ctx[c]
        b, qq, kk = k['b'], k['qq'], k['kk']
        r1, r2, r3, r4 = (b[j * B - 1:j * B, :] for j in (1, 2, 3, 4))
        start = jnp.where(row < B, 0.0, jnp.where(row < 2 * B, r1, jnp.where(row < 3 * B, r2, r3)))
        mid = jnp.where(row < 2 * B, r1, r3)
        bd = b - start
        ops = dict(
            qd=qq * jnp.exp(bd), kd=kk * jnp.exp(-bd),
            qm=qq * jnp.exp(jnp.minimum(b - mid, 0.0)), km=kk * jnp.exp(jnp.minimum(mid - b, 0.0)),
            qf=qq * jnp.exp(jnp.minimum(b - r2, 0.0)), kf=kk * jnp.exp(jnp.minimum(r2 - b, 0.0)),
            qs=qq * jnp.exp(b), ks=kk * jnp.exp(r4 - b))
        k['ops'] = {name: v.astype(BF16) for name, v in ops.items()}
        k['e4'] = jnp.exp(r4)
        k['vs'] = [vi[c * L:(c + 1) * L, hl].astype(BF16) for hl in heads]

    def scores(c):
        k = ctx[c]
        ops = k['ops']
        k['a_d'] = [_dot_nt(ops['qd'][:, hl], ops['kd'][:, hl]) for hl in heads]
        k['a_m'] = [_dot_nt(ops['qm'][:, hl], ops['km'][:, hl]) for hl in heads]
        k['a_f'] = [_dot_nt(ops['qf'][:, hl], ops['kf'][:, hl]) for hl in heads]

    def masks(c):
        k = ctx[c]
        k['a'] = [(jnp.where(m_diag, k['a_d'][h], 0.0) + jnp.where(m_mid, k['a_m'][h], 0.0)
                   + jnp.where(m_far, k['a_f'][h], 0.0)).astype(BF16) for h in range(HG_HEADS)]

    def outputs(c):
        k = ctx[c]
        ops, vs = k['ops'], k['vs']
        k['s_old'] = [st_ref[h] for h in range(HG_HEADS)]
        k['o'] = [_dot(k['a'][h], vs[h]) + _dot_nt(ops['qs'][:, heads[h]], k['s_old'][h].astype(BF16))
                  for h in range(HG_HEADS)]
        k['ds'] = [_dot_tn(vs[h], ops['ks'][:, hl]) for h, hl in enumerate(heads)]

    def finish_chunk(c):
        k = ctx[c]
        for h, hl in enumerate(heads):
            st_ref[h] = k['e4'][:, hl] * k['s_old'][h] + k['ds'][h]
        for h, hl in enumerate(heads):
            o = k['o'][h]
            rms = lax.rsqrt(jnp.mean(o * o, -1, keepdims=True) + 1e-6)
            gv = gv_all[c * L:(c + 1) * L, hl]
            y_ref[0, c * L:(c + 1) * L, hl] = (o * rms * gn_ref[:, hl] * (gv * _sigmoid(gv))).astype(y_ref.dtype)

    def finish():
        @pl.when(last)
        def _():
            for h in range(HG_HEADS):
                s_ref[0, h] = st_ref[h].T

    stages = (gates, cumsum, decays, scores, masks, outputs, finish_chunk)
    n_chunk, skew = tt // L, 2
    steps = []
    for slot in range(len(stages) + skew * (n_chunk - 1)):
        for c in range(n_chunk):
            si = slot - skew * c
            if 0 <= si < len(stages):
                steps.append(functools.partial(stages[si], c))
    return steps + [finish]


RG_PAD = 8
RG_PIECE = 64


def _rglru_tile(x, g, first, cw_ref, cb_ref, wa_ref, ba_ref, wx_ref, bx_ref, lam_ref,
                h0_ref, buf0_ref, y_ref, h_ref, buf_ref, xp_ref):
    tt = x.shape[0]
    hist = RG_CONV - 1

    @pl.when(first)
    def _():
        h_ref[...] = h0_ref[...]
        xp_ref[RG_PAD - hist:RG_PAD, :] = buf0_ref[0]

    xp_ref[RG_PAD:RG_PAD + tt, :] = x[...]
    n_piece = max(1, tt // RG_PIECE)
    rows = tt // n_piece
    softplus_neg = jax.nn.softplus(-lam_ref[...])
    sub = lax.broadcasted_iota(jnp.int32, (1, SUBLANE, 1), 1)

    ctx = [dict() for _ in range(n_piece)]

    def conv(i):
        r0 = i * rows
        xr = x[r0:r0 + rows]
        u = cb_ref[...] + xr * cw_ref[hist:hist + 1, :]
        for j in range(hist):
            u = u + xp_ref[RG_PAD - hist + j + r0:RG_PAD - hist + j + r0 + rows, :] * cw_ref[j:j + 1, :]
        ctx[i]['u'] = u

    def gate_dots(i):
        ub = ctx[i]['u'].astype(BF16)
        ctx[i]['ra'] = _dot(ub, wa_ref[...])
        ctx[i]['rx'] = _dot(ub, wx_ref[...])

    def scan(i):
        r0 = i * rows
        u = ctx[i]['u']
        r = _sigmoid(ctx[i]['ra'] + ba_ref[...])
        ig = _sigmoid(ctx[i]['rx'] + bx_ref[...])
        log_a = -RG_C * r * softplus_neg
        a = jnp.exp(log_a)
        bt = jnp.sqrt(jnp.maximum(-jnp.tanh(log_a) * (a * a + 1.0), 0.0)) * (ig * u)
        n_grp = rows // SUBLANE
        a3 = a.reshape(n_grp, SUBLANE, RG_W)
        b3 = bt.reshape(n_grp, SUBLANE, RG_W)
        s = 1
        while s < SUBLANE:
            keep = sub >= s
            a_sh = jnp.where(keep, pltpu.roll(a3, s, 1), 1.0)
            b_sh = jnp.where(keep, pltpu.roll(b3, s, 1), 0.0)
            b3 = a3 * b_sh + b3
            a3 = a3 * a_sh
            s *= 2
        h_in = h_ref[0]
        hs = []
        for gi in range(n_grp):
            hs.append(a3[gi] * h_in + b3[gi])
            h_in = hs[-1][SUBLANE - 1:SUBLANE, :]
        h_ref[0] = h_in
        y_ref[0, r0:r0 + rows, :] = (jnp.concatenate(hs, axis=0) * _gelu(g[r0:r0 + rows])).astype(y_ref.dtype)

    def finish():
        new_buf = xp_ref[RG_PAD + tt - hist:RG_PAD + tt, :]
        xp_ref[RG_PAD - hist:RG_PAD, :] = new_buf
        buf_ref[0] = new_buf

    stages = (conv, gate_dots, scan)
    steps = []
    for slot in range(len(stages) + n_piece - 1):
        for i in range(n_piece):
            if 0 <= slot - i < len(stages):
                steps.append(functools.partial(stages[slot - i], i))
    return steps + [finish]


def _merge_kernel(x_ref, y0_ref, y1_ref, y2_ref, y3_ref, wmg_ref, bmg_ref, wbr_ref, wout_ref,
                  g_ref, b_ref, o_ref, *, alpha):
    x = x_ref[...]
    xb = x.astype(BF16)
    mix = None
    for m, y_ref in enumerate((y0_ref, y1_ref, y2_ref, y3_ref)):
        gate = _sigmoid(_dot(xb, wmg_ref[m]) + bmg_ref[m])
        term = gate * _dot(y_ref[...], wbr_ref[m])
        mix = term if mix is None else mix + term
    o_ref[...] = _layer_norm(alpha * x + _dot(mix.astype(BF16), wout_ref[...]), g_ref[...], b_ref[...])


def _merge(x2d, ys, wmg, bmg, wbr, wout, g, bb, alpha):
    m = x2d.shape[0]
    tm = min(m, DENSE_ROWS)
    row = lambda width: pl.BlockSpec((tm, width), lambda i: (i, 0))
    c2 = lambda shape: pl.BlockSpec(shape, lambda i: (0, 0))
    c3 = lambda shape: pl.BlockSpec(shape, lambda i: (0, 0, 0))
    return pl.pallas_call(
        functools.partial(_merge_kernel, alpha=alpha),
        grid=(m // tm,),
        in_specs=[row(D_MODEL)] + [row(BR_W)] * N_BRANCH +
                 [c3((N_BRANCH, D_MODEL, D_MODEL)), c3((N_BRANCH, 1, D_MODEL)),
                  c3((N_BRANCH, BR_W, D_MODEL)), c2((D_MODEL, D_MODEL)),
                  c2((1, D_MODEL)), c2((1, D_MODEL))],
        out_specs=row(D_MODEL),
        out_shape=jax.ShapeDtypeStruct((m, D_MODEL), F32),
        compiler_params=_cparams("arbitrary"),
        name="merge",
    )(x2d, *ys, wmg, bmg, wbr, wout, g, bb)


FF_PAD = 8
FF_COL_BLK = 512


def _ffn_kernel(x_ref, wg_ref, wu_ref, cw_ref, cb_ref, wd_ref, g_ref, b_ref, buf0_ref,
                o_ref, buf_ref, gp_ref, *, alpha):
    tt = x_ref.shape[1]
    hist = FFN_CONV - 1

    @pl.when(pl.program_id(1) == 0)
    def _():
        gp_ref[FF_PAD - hist:FF_PAD, :] = buf0_ref[0]

    x = x_ref[0]
    xb = x.astype(BF16)
    chunks = [slice(c0, c0 + FF_COL_BLK) for c0 in range(0, D_FF, FF_COL_BLK)]

    def in_proj(cols):
        gp = _dot(xb, wg_ref[:, cols])
        gp_ref[FF_PAD:FF_PAD + tt, cols] = gp
        return gp, _dot(xb, wu_ref[:, cols])

    def out_proj(cols, gp, up):
        gc = cb_ref[:, cols] + gp * cw_ref[hist:hist + 1, cols]
        for j in range(hist):
            gc = gc + gp_ref[FF_PAD - hist + j:FF_PAD - hist + j + tt, cols] * cw_ref[j:j + 1, cols]
        return _dot((_gelu(gc) * up).astype(BF16), wd_ref[cols, :])

    acc = None
    pending = in_proj(chunks[0])
    for i, cols in enumerate(chunks):
        nxt = in_proj(chunks[i + 1]) if i + 1 < len(chunks) else None
        part = out_proj(cols, *pending)
        acc = part if acc is None else acc + part
        pending = nxt
    new_buf = gp_ref[FF_PAD + tt - hist:FF_PAD + tt, :]
    gp_ref[FF_PAD - hist:FF_PAD, :] = new_buf
    buf_ref[0] = new_buf
    o_ref[0] = _layer_norm(alpha * x + acc, g_ref[...], b_ref[...])


def _ffn(x3, wg, wu, cw, cb, wd, g, bb, buf0, alpha):
    b, t, _ = x3.shape
    tt = min(t, DENSE_ROWS)
    nt = t // tt
    c2 = lambda shape: pl.BlockSpec(shape, lambda i, j: (0, 0))
    buf_spec = pl.BlockSpec((1, FFN_CONV - 1, D_FF), lambda i, j: (i, 0, 0))
    seq = pl.BlockSpec((1, tt, D_MODEL), lambda i, j: (i, j, 0))
    return pl.pallas_call(
        functools.partial(_ffn_kernel, alpha=alpha),
        grid=(b, nt),
        in_specs=[seq, c2((D_MODEL, D_FF)), c2((D_MODEL, D_FF)), c2((FFN_CONV, D_FF)), c2((1, D_FF)),
                  c2((D_FF, D_MODEL)), c2((1, D_MODEL)), c2((1, D_MODEL)), buf_spec],
        out_specs=[seq, buf_spec],
        out_shape=[jax.ShapeDtypeStruct(x3.shape, F32), jax.ShapeDtypeStruct(buf0.shape, F32)],
        scratch_shapes=[pltpu.VMEM((FF_PAD + tt, D_FF), F32)],
        compiler_params=_cparams("arbitrary", "arbitrary"),
        name="ffn",
    )(x3, wg, wu, cw, cb, wd, g, bb, buf0)


def _prep_layer(l, w_in, b_in, ml_norm_g, hg_norm_g, lbs, rg_conv_w, rg_conv_b, rg_w_a, rg_b_a,
                rg_w_x, rg_b_x, rg_lambda, w_mg, b_mg, w_br, w_out, ln1_g, ln1_b,
                w_ff_gate, w_ff_up, ff_conv_w, ff_conv_b, w_ff_down, ln2_g, ln2_b):
    offs = [int(o) for o in np.concatenate([[0], np.cumsum(PROJ_SIZES)])]
    wl, bl = w_in[l].astype(BF16), b_in[l]
    cols = lambda a, segs: [a[..., offs[s]:offs[s + 1]] for s in segs]
    w_small = jnp.concatenate(cols(wl, SMALL_SEGS), axis=1)
    b_small = jnp.concatenate(cols(bl, SMALL_SEGS))
    w = jnp.concatenate(cols(wl, MAIN_SEGS) + [w_small, jnp.zeros((D_MODEL, SMALL_W - N_SMALL), BF16)]
                        + cols(wl, STASH_SEGS) + cols(wl, KV_SEGS), axis=1)
    bvec = jnp.concatenate(cols(bl, MAIN_SEGS) + [b_small, jnp.zeros((SMALL_W - N_SMALL,), F32)]
                           + cols(bl, STASH_SEGS) + cols(bl, KV_SEGS))[None, :]
    wst = w_small.T
    bst = b_small[:, None]

    def block_diag(wb):
        eye = jnp.eye(RG_BLOCKS, dtype=F32)
        return jnp.einsum('nde,nm->ndme', wb, eye).reshape(RG_W, RG_W).astype(BF16)

    row = lambda v: v[None, :].astype(F32)
    return dict(
        w=w, b=bvec, wst=wst, bst=bst, ml_g=row(ml_norm_g[l]), hg_g=row(hg_norm_g[l]), lb=row(lbs[l]),
        rg_cw=rg_conv_w[l], rg_cb=row(rg_conv_b[l]), rg_wa=block_diag(rg_w_a[l]), rg_ba=row(rg_b_a[l]),
        rg_wx=block_diag(rg_w_x[l]), rg_bx=row(rg_b_x[l]), rg_lam=row(rg_lambda[l]),
        wmg=w_mg[l].astype(BF16), bmg=b_mg[l][:, None, :], wbr=w_br[l].astype(BF16),
        wout=w_out[l].astype(BF16), ln1_g=row(ln1_g[l]), ln1_b=row(ln1_b[l]),
        wg=w_ff_gate[l].astype(BF16), wu=w_ff_up[l].astype(BF16), ff_cw=ff_conv_w[l],
        ff_cb=row(ff_conv_b[l]), wd=w_ff_down[l].astype(BF16), ln2_g=row(ln2_g[l]), ln2_b=row(ln2_b[l]))


def _kv_t_minor(t):
    return t >= LANE


def _trunk_layer(x, p, fox_past, ml_c, ml_n, ml_m, hg_s, rg_h, rg_buf, ff_buf, alpha, layer, depth, kv_bufs):
    b, t, _ = x.shape
    assert t % 64 == 0 and (t <= 256 or t % 256 == 0)
    x2d = x.reshape(b * t, D_MODEL)
    rg = (p['rg_cw'], p['rg_cb'], p['rg_wa'], p['rg_ba'], p['rg_wx'], p['rg_bx'], p['rg_lam'],
          rg_h.reshape(b, 1, RG_W), rg_buf)
    hg = (p['lb'], p['hg_g'], hg_s)
    ml = (ml_c, ml_n, ml_m.reshape(b, 1, ML_HEADS), p['ml_g'])
    (pa3, k32, v32, kb, vt, st, y_hg, hg_s, y_rg, rg_h, rg_buf, y_ml, ml_c, ml_n, ml_m) = _proj(
        x, p['w'], p['b'], p['wst'], p['bst'], hg, rg, ml, layer, depth, _kv_t_minor(t), kv_bufs)
    past = None
    if fox_past is not None:
        k_cache, v_cache, logf_past = fox_past
        to_feature_major = lambda a: jnp.transpose(a, (0, 1, 3, 4, 2)).reshape(depth, b, BR_W, a.shape[2])
        past = (to_feature_major(k_cache), to_feature_major(v_cache), logf_past)
    y_fx, f_log = _fox(pa3, st, kb, vt, past, layer)
    f_log = jnp.swapaxes(f_log, 1, 2)
    ys = [y.reshape(b * t, BR_W) for y in (y_ml, y_fx, y_hg, y_rg)]
    x1 = _merge(x2d, ys, p['wmg'], p['bmg'], p['wbr'], p['wout'], p['ln1_g'], p['ln1_b'], alpha)
    x2, ff_buf = _ffn(x1.reshape(b, t, D_MODEL), p['wg'], p['wu'], p['ff_cw'], p['ff_cb'], p['wd'],
                      p['ln2_g'], p['ln2_b'], ff_buf, alpha)
    state = (f_log, ml_c, ml_n, ml_m.reshape(b, ML_HEADS), hg_s, rg_h.reshape(b, RG_W), rg_buf, ff_buf)
    return x2, (k32, v32), state


def _kv_output(buf, t):
    depth, b = buf.shape[:2]
    if _kv_t_minor(t):
        return jnp.transpose(buf.reshape(depth, b, FX_HEADS, FX_HD, t), (0, 1, 4, 2, 3))
    return buf.reshape(depth, b, t, FX_HEADS, FX_HD)


def kernel(x_prompt, x_sample, cache_fox_k, cache_fox_v, cache_fox_logf, state_mlstm_c, state_mlstm_n,
           state_mlstm_m, state_hgrn_s, state_rglru_h, state_rglru_conv, state_ffn_conv,
           w_in, b_in, ml_norm_g, hg_norm_g, hg_lb_logits, rg_conv_w, rg_conv_b, rg_w_a, rg_b_a,
           rg_w_x, rg_b_x, rg_lambda, w_mg, b_mg, w_br, w_out, ln1_g, ln1_b,
           w_ff_gate, w_ff_up, ff_conv_w, ff_conv_b, w_ff_down, ln2_g, ln2_b):
    depth = w_in.shape[0]
    alpha = (2 * depth) ** 0.25
    pl_soft = jax.nn.softmax(hg_lb_logits.astype(F32), axis=0)
    lbs = jnp.cumsum(pl_soft, axis=0) - pl_soft[0]
    bp = x_prompt.shape[0]
    yp, ys = x_prompt, x_sample
    p_new, s_new = [], []
    kv_p, kv_s = (), ()
    for l in range(depth):
        p = _prep_layer(l, w_in, b_in, ml_norm_g, hg_norm_g, lbs, rg_conv_w, rg_conv_b, rg_w_a, rg_b_a,
                        rg_w_x, rg_b_x, rg_lambda, w_mg, b_mg, w_br, w_out, ln1_g, ln1_b,
                        w_ff_gate, w_ff_up, ff_conv_w, ff_conv_b, w_ff_down, ln2_g, ln2_b)
        z = lambda *shape: jnp.zeros((bp,) + shape, F32)
        yp, kv_p, st_p = _trunk_layer(yp, p, None, z(ML_HEADS, ML_DK, ML_DV), z(ML_HEADS, ML_DK), z(ML_HEADS),
                                      z(HG_HEADS, HG_DK, HG_DK), z(RG_W), z(RG_CONV - 1, RG_W),
                                      z(FFN_CONV - 1, D_FF), alpha, l, depth, kv_p if l else ())
        p_new.append(st_p)
        ys, kv_s, st_s = _trunk_layer(ys, p, (cache_fox_k, cache_fox_v, cache_fox_logf[l]),
                                      state_mlstm_c[l], state_mlstm_n[l], state_mlstm_m[l], state_hgrn_s[l],
                                      state_rglru_h[l], state_rglru_conv[l], state_ffn_conv[l], alpha,
                                      l, depth, kv_s if l else ())
        s_new.append(st_s)
    n_st = len(p_new[0])
    p_out = [jnp.stack([st[j] for st in p_new]) for j in range(n_st)]
    s_out = [jnp.stack([st[j] for st in s_new]) for j in range(n_st)]
    tp, ts = x_prompt.shape[1], x_sample.shape[1]
    return (yp, ys, _kv_output(kv_p[0], tp), _kv_output(kv_p[1], tp), *p_out,
            _kv_output(kv_s[0], ts), _kv_output(kv_s[1], ts), *s_out)
```

```python
import functools

import numpy as np
import jax
import jax.numpy as jnp
from jax import lax
from jax.experimental import pallas as pl
from jax.experimental.pallas import tpu as pltpu

F32 = jnp.float32
BF16 = jnp.bfloat16

D_MODEL = 1024
BR_W = D_MODEL // 2
ML_HEADS = 4
ML_DV = BR_W // ML_HEADS
ML_DK = ML_DV // 2
ML_CHUNK = 128
FX_HD = 64
FX_HEADS = BR_W // FX_HD
FX_PAST_TILE = 256
FX_KEY_TILE = 256
LOG2E = 1.4426950408889634
HG_HEADS = 4
HG_DK = BR_W // HG_HEADS
HG_BLOCK = 16
HG_CHUNK = 64
RG_W = BR_W
RG_BLOCKS = 8
RG_BD = RG_W // RG_BLOCKS
RG_CONV = 4
RG_C = 8.0
D_FF = 2 * D_MODEL
FFN_CONV = 3
N_BRANCH = 4
PROJ_SIZES = (ML_HEADS * ML_DK, ML_HEADS * ML_DK, BR_W, BR_W, ML_HEADS, ML_HEADS,
              BR_W, BR_W, BR_W, FX_HEADS,
              BR_W, BR_W, BR_W, BR_W,
              RG_W, RG_W)
NEG_BIG = -1e30
TINY = 1e-30

LANE = 128
SUBLANE = 8
SMALL_W = LANE
N_SMALL = 2 * ML_HEADS + FX_HEADS
MAIN_SEGS = (6,)
SMALL_SEGS = (4, 5, 9)
STASH_SEGS = (10, 11, 12, 13, 14, 15, 0, 1, 2, 3)
KV_SEGS = (7, 8)
N_MAIN = sum(PROJ_SIZES[i] for i in MAIN_SEGS)
PA_W = N_MAIN + SMALL_W
STASH_COL0 = PA_W
KV_COL0 = STASH_COL0 + sum(PROJ_SIZES[i] for i in STASH_SEGS)
COL_BLK = 512
DENSE_ROWS = 512
VMEM_LIMIT = 56 * 1024 * 1024


def _cparams(*sem):
    return pltpu.CompilerParams(dimension_semantics=sem, vmem_limit_bytes=VMEM_LIMIT)


def _dot(a, b):
    return jnp.dot(a, b, preferred_element_type=F32)


def _dot_nt(a, b):
    return lax.dot_general(a, b, (((1,), (1,)), ((), ())), preferred_element_type=F32)


def _dot_tn(a, b):
    return lax.dot_general(a, b, (((0,), (0,)), ((), ())), preferred_element_type=F32)


def _split3(x):
    hi = x.astype(BF16)
    r1 = x - hi.astype(F32)
    mid = r1.astype(BF16)
    lo = (r1 - mid.astype(F32)).astype(BF16)
    return hi, mid, lo


def _tri_dot(tri, x):
    t = tri.astype(BF16)
    hi, mid, lo = _split3(x)
    return (_dot(t, lo) + _dot(t, mid)) + _dot(t, hi)


def _dot_tri(x, tri):
    t = tri.astype(BF16)
    hi, mid, lo = _split3(x)
    return (_dot(lo, t) + _dot(mid, t)) + _dot(hi, t)


def _tri(n, lower, block=None):
    r = lax.broadcasted_iota(jnp.int32, (n, n), 0)
    c = lax.broadcasted_iota(jnp.int32, (n, n), 1)
    m = (r >= c) if lower else (r <= c)
    if block is not None:
        m = m & ((r // block) == (c // block))
    return m.astype(F32)


def _dot_hi_nt(a, b):
    return lax.dot_general(a, b, (((1,), (1,)), ((), ())), precision=lax.Precision.HIGHEST,
                           preferred_element_type=F32)


_log_sigmoid = jax.nn.log_sigmoid
_sigmoid = jax.nn.sigmoid
_gelu = jax.nn.gelu


def _layer_norm(v, g, b):
    mu = jnp.mean(v, -1, keepdims=True)
    c = v - mu
    var = jnp.mean(c * c, -1, keepdims=True)
    return c * lax.rsqrt(var + 1e-5) * g + b


N_RG_IN = 9
N_HG_IN = 3
N_ML_IN = 4
N_PROJ_IN = 5 + N_HG_IN + N_RG_IN + N_ML_IN
STASH_W = 9 * COL_BLK + SMALL_W


def _proj_kernel(*refs, t_minor, n_alias, nt):
    x_ref, w_ref, b_ref, wst_ref, bst_ref = refs[:5]
    hg_in = refs[5:5 + N_HG_IN]
    rg_in = refs[5 + N_HG_IN:5 + N_HG_IN + N_RG_IN]
    ml_in = refs[5 + N_HG_IN + N_RG_IN:N_PROJ_IN]
    (pa_ref, k32_ref, v32_ref, kb_ref, vb_ref, st_ref,
     yhg_ref, hgs_ref, yrg_ref, rgh_ref, rgbuf_ref, yml_ref, mlc_ref, mln_ref, mlm_ref,
     hgst_ref, xp_ref, mlct_ref, stash_ref, stash_t_ref) = refs[N_PROJ_IN + n_alias:]
    s = pl.program_id(0)
    slot_w = lax.rem(s, 2)
    slot_r = 1 - slot_w
    t_prev = lax.rem(jnp.maximum(s - 1, 0), nt)
    first, last = t_prev == 0, t_prev == nt - 1

    @pl.when(s == 0)
    def _():
        def zero_rows(i, carry):
            stash_ref[1, pl.ds(pl.multiple_of(i * SUBLANE, SUBLANE), SUBLANE), :] = jnp.zeros((SUBLANE, STASH_W), F32)
            return carry
        lax.fori_loop(0, stash_ref.shape[1] // SUBLANE, zero_rows, 0)
        stash_t_ref[1] = jnp.zeros(stash_t_ref.shape[1:], F32)

    xb = x_ref[0].astype(BF16)
    col = lambda c0, width: _dot(xb, w_ref[:, c0:c0 + width]) + b_ref[:, c0:c0 + width]
    stashed = lambda c0, width: stash_ref.at[slot_r, :, c0:c0 + width]
    blk = lambda j: stashed(j * COL_BLK, COL_BLK)
    hg_steps = _hgrn_tile(blk(0), blk(1), blk(2), blk(3), first, last, *hg_in, yhg_ref, hgs_ref, hgst_ref)
    rg_steps = _rglru_tile(blk(4), blk(5), first, *rg_in, yrg_ref, rgh_ref, rgbuf_ref, xp_ref)
    qk0 = 6 * COL_BLK
    ml_steps = _mlstm_tile(stashed(qk0, ML_HEADS * ML_DK), stashed(qk0 + ML_HEADS * ML_DK, ML_HEADS * ML_DK),
                           blk(7), blk(8), stashed(9 * COL_BLK, SMALL_W), stash_t_ref.at[slot_r],
                           first, last, *ml_in, yml_ref, mlc_ref, mln_ref, mlm_ref, mlct_ref)

    def store_stash(j):
        stash_ref[slot_w, :, j * COL_BLK:(j + 1) * COL_BLK] = col(STASH_COL0 + j * COL_BLK, COL_BLK)

    def store_main(c0):
        width = min(COL_BLK, PA_W - c0)
        val = col(c0, width)
        pa_ref[0, :, c0:c0 + width] = val
        if c0 == N_MAIN:
            stash_ref[slot_w, :, 9 * COL_BLK:] = val

    def store_rows():
        st = _dot_nt(wst_ref[...], xb) + bst_ref[...]
        st_ref[0] = st
        stash_t_ref[slot_w] = st

    def store_k():
        k = col(KV_COL0, BR_W)
        kb_ref[0] = k.astype(BF16)
        k32_ref[0, 0] = k.T if t_minor else k

    def store_v():
        v = col(KV_COL0 + BR_W, BR_W)
        vt = v.T
        vb_ref[0] = vt.astype(BF16)
        v32_ref[0, 0] = vt if t_minor else v

    proj_steps = ([functools.partial(store_main, c0) for c0 in range(0, PA_W, COL_BLK)]
                  + [store_rows, store_k, store_v] + [functools.partial(store_stash, j) for j in range(9)])
    tagged = [((i + (-0.5 if steps is proj_steps else 0.5)) / len(steps), steps[i])
              for steps in (hg_steps, ml_steps, rg_steps, proj_steps) for i in range(len(steps))]
    for _, step in sorted(tagged, key=lambda ps: ps[0]):
        step()


def _proj(x3, w, b, wst, bst, hg, rg, ml, layer, depth, t_minor, kv_bufs):
    bsz, t, _ = x3.shape
    tt = min(t, 256)
    nt = t // tt
    n_tiles = bsz * nt
    nw = w.shape[1]
    cur = lambda s: divmod(jnp.minimum(s, n_tiles - 1), nt)
    prev = lambda s: divmod(jnp.maximum(s - 1, 0), nt)
    seq = lambda width: pl.BlockSpec((1, tt, width), lambda s: (*cur(s), 0))
    seq_prev = lambda width: pl.BlockSpec((1, tt, width), lambda s: (*prev(s), 0))
    const = lambda shape: pl.BlockSpec(shape, lambda s: (0, 0))
    sds = lambda width, dt: jax.ShapeDtypeStruct((bsz, t, width), dt)
    vec = const((1, RG_W))
    mat = const((RG_W, RG_W))
    h_spec = pl.BlockSpec((1, 1, RG_W), lambda s: (prev(s)[0], 0, 0))
    buf_spec = pl.BlockSpec((1, RG_CONV - 1, RG_W), lambda s: (prev(s)[0], 0, 0))
    rg_specs = [const((RG_CONV, RG_W)), vec, mat, vec, mat, vec, vec, h_spec, buf_spec]
    s_spec = pl.BlockSpec((1, HG_HEADS, HG_DK, HG_DK), lambda s: (prev(s)[0], 0, 0, 0))
    hg_specs = [vec, vec, s_spec]
    c_spec = pl.BlockSpec((1, ML_HEADS, ML_DK, ML_DV), lambda s: (prev(s)[0], 0, 0, 0))
    n_spec = pl.BlockSpec((1, ML_HEADS, ML_DK), lambda s: (prev(s)[0], 0, 0))
    m_spec = pl.BlockSpec((1, 1, ML_HEADS), lambda s: (prev(s)[0], 0, 0))
    ml_specs = [c_spec, n_spec, m_spec, vec]
    assert len(rg) == N_RG_IN and len(hg) == N_HG_IN and len(ml) == N_ML_IN
    if t_minor:
        kv_spec = pl.BlockSpec((1, 1, BR_W, tt), lambda s: (layer, cur(s)[0], 0, cur(s)[1]))
        kv_sds = jax.ShapeDtypeStruct((depth, bsz, BR_W, t), F32)
    else:
        kv_spec = pl.BlockSpec((1, 1, tt, BR_W), lambda s: (layer, *cur(s), 0))
        kv_sds = jax.ShapeDtypeStruct((depth, bsz, t, BR_W), F32)
    n_alias = len(kv_bufs)
    return pl.pallas_call(
        functools.partial(_proj_kernel, t_minor=t_minor, n_alias=n_alias, nt=nt),
        grid=(n_tiles + 1,),
        in_specs=[seq(D_MODEL), const((D_MODEL, nw)), const((1, nw)),
                  const((2 * 8, D_MODEL)), const((2 * 8, 1))] + hg_specs + rg_specs + ml_specs
                 + [pl.BlockSpec(memory_space=pl.ANY)] * n_alias,
        out_specs=[seq(PA_W), kv_spec, kv_spec, seq(BR_W),
                   pl.BlockSpec((1, BR_W, tt), lambda s: (cur(s)[0], 0, cur(s)[1])),
                   pl.BlockSpec((1, 2 * 8, tt), lambda s: (cur(s)[0], 0, cur(s)[1])),
                   seq_prev(BR_W), s_spec, seq_prev(RG_W), h_spec, buf_spec,
                   seq_prev(BR_W), c_spec, n_spec, m_spec],
        out_shape=[sds(PA_W, F32), kv_sds, kv_sds, sds(BR_W, BF16),
                   jax.ShapeDtypeStruct((bsz, BR_W, t), BF16),
                   jax.ShapeDtypeStruct((bsz, 2 * 8, t), F32),
                   sds(BR_W, BF16), jax.ShapeDtypeStruct((bsz, HG_HEADS, HG_DK, HG_DK), F32),
                   sds(RG_W, BF16), jax.ShapeDtypeStruct((bsz, 1, RG_W), F32),
                   jax.ShapeDtypeStruct((bsz, RG_CONV - 1, RG_W), F32),
                   sds(BR_W, BF16), jax.ShapeDtypeStruct((bsz, ML_HEADS, ML_DK, ML_DV), F32),
                   jax.ShapeDtypeStruct((bsz, ML_HEADS, ML_DK), F32),
                   jax.ShapeDtypeStruct((bsz, 1, ML_HEADS), F32)],
        input_output_aliases={N_PROJ_IN + a: 1 + a for a in range(n_alias)},
        scratch_shapes=[pltpu.VMEM((HG_HEADS, HG_DK, HG_DK), F32), pltpu.VMEM((RG_PAD + tt, RG_W), F32),
                        pltpu.VMEM((ML_HEADS, ML_DV, ML_DK), F32),
                        pltpu.VMEM((2, tt, STASH_W), F32), pltpu.VMEM((2, 2 * 8, tt), F32)],
        compiler_params=_cparams("arbitrary"),
        name="proj",
    )(x3, w, b, wst, bst, *hg, *rg, *ml, *kv_bufs)


def _mlstm_tile(q_ref, k_ref, v_ref, o_ref, sm_ref, st_ref, first, last, c0_ref, n0_ref, m0_ref, g_ref,
                y_ref, c_ref, n_ref, m_ref, ct_ref):
    tt = q_ref.shape[0]
    L = min(tt, ML_CHUNK)
    H = range(ML_HEADS)

    @pl.when(first)
    def _():
        for h in H:
            ct_ref[h] = c0_ref[0, h].T
        n_ref[...] = n0_ref[...]
        m_ref[...] = m0_ref[...]

    tri_l = _tri(L, True)
    tri_u = _tri(L, False)
    causal_t = tri_u > 0.5
    dk = [slice(h * ML_DK, (h + 1) * ML_DK) for h in H]
    dv = [slice(h * ML_DV, (h + 1) * ML_DV) for h in H]
    n_chunk = tt // L
    ctx = [dict() for _ in range(n_chunk)]

    def gates(c):
        k = ctx[c]
        rows = slice(c * L, (c + 1) * L)
        sm = sm_ref[rows, :]
        st = st_ref[:, rows]
        k['i_col'], k['lf_col'] = sm[:, 0:ML_HEADS], _log_sigmoid(sm[:, ML_HEADS:2 * ML_HEADS])
        k['lf_row'] = _log_sigmoid(st[ML_HEADS:2 * ML_HEADS, :])
        k['qf'] = [q_ref[rows, dk[h]] for h in H]
        k['qb'] = [x.astype(BF16) for x in k['qf']]
        k['kf'] = [k_ref[rows, dk[h]] * (ML_DK ** -0.5) for h in H]
        k['vb'] = [v_ref[rows, dv[h]].astype(BF16) for h in H]

    def cumsums(c):
        k = ctx[c]
        k['a_cols'] = k['i_col'] - _tri_dot(tri_l, k['lf_col'])
        k['fcum_row'] = _dot_tri(k['lf_row'], tri_u)
        k['s_t'] = [_dot_nt(k['kf'][h].astype(BF16), k['qb'][h]) for h in H]

    def weights(c):
        k = ctx[c]
        m_all = m_ref[0]
        k['w_t'], k['g'], k['m_t'], k['ks'] = [], [], [], []
        for h in H:
            fr = k['fcum_row'][h:h + 1, :]
            a_col = k['a_cols'][:, h:h + 1]
            d_t = jnp.where(causal_t, fr + a_col, NEG_BIG)
            prev = fr + m_all[:, h:h + 1]
            m_t = jnp.maximum(prev, jnp.max(d_t, 0, keepdims=True))
            k['m_t'].append(m_t)
            k['w_t'].append(jnp.exp(d_t - m_t) * k['s_t'][h])
            k['g'].append(jnp.exp(prev - m_t))
            k['ks'].append(jnp.exp(fr[:, L - 1:L] + a_col - m_t[:, L - 1:L]) * k['kf'][h])
        head = lax.broadcasted_iota(jnp.int32, (1, ML_HEADS), 1)
        m_vec = k['m_t'][0][:, L - 1:L]
        for h in range(1, ML_HEADS):
            m_vec = jnp.where(head == h, k['m_t'][h][:, L - 1:L], m_vec)
        m_ref[0] = m_vec

    def products(c):
        k = ctx[c]
        k['ct'] = [ct_ref[h] for h in H]
        k['nst'] = [n_ref[0, h:h + 1, :] for h in H]
        k['num_t'] = [_dot_tn(k['vb'][h], k['w_t'][h].astype(BF16)) for h in H]
        k['qc_t'] = [_dot_nt(k['ct'][h].astype(BF16), k['qb'][h]) for h in H]
        k['qn'] = [_dot_hi_nt(k['nst'][h], k['qf'][h]) for h in H]
        k['dct'] = [_dot_tn(k['vb'][h], k['ks'][h].astype(BF16)) for h in H]

    def finish_chunk(c):
        k = ctx[c]
        rows = slice(c * L, (c + 1) * L)
        for h in H:
            g_last = k['g'][h][:, L - 1:L]
            ct_ref[h] = g_last * k['ct'][h] + k['dct'][h]
            n_ref[0, h:h + 1, :] = g_last * k['nst'][h] + jnp.sum(k['ks'][h], 0, keepdims=True)
        for h in H:
            g, m_t = k['g'][h], k['m_t'][h]
            den = jnp.sum(k['w_t'][h], 0, keepdims=True) + g * k['qn'][h]
            h_t = (k['num_t'][h] + g * k['qc_t'][h]) / jnp.maximum(jnp.abs(den), jnp.exp(-m_t))
            mu = jnp.mean(h_t, 0, keepdims=True)
            hc = h_t - mu
            var = jnp.mean(hc * hc, 0, keepdims=True)
            hn = (hc * lax.rsqrt(var + 1e-5)).T
            yv = _sigmoid(o_ref[rows, dv[h]]) * (hn * g_ref[:, dv[h]])
            y_ref[0, rows, dv[h]] = yv.astype(y_ref.dtype)

    def finish():
        @pl.when(last)
        def _():
            for h in H:
                c_ref[0, h] = ct_ref[h].T

    stages = (gates, cumsums, weights, products, finish_chunk)
    steps = []
    skew = 2
    for slot in range(len(stages) + skew * (n_chunk - 1)):
        for c in range(n_chunk):
            if 0 <= slot - skew * c < len(stages):
                steps.append(functools.partial(stages[slot - skew * c], c))
    return steps + [finish]


def _fox_kernel(*refs, n_past, tk):
    if n_past:
        (q_ref, sm_ref, st_ref, kb_ref, vt_ref, kp_ref, vp_ref, lfp_ref,
         y_ref, lf_ref, ccol_ref, crun_ref, rrun_ref, yt_ref, qm_ref, m_ref, l_ref) = refs
    else:
        (q_ref, sm_ref, st_ref, kb_ref, vt_ref,
         y_ref, lf_ref, ccol_ref, crun_ref, rrun_ref, yt_ref, qm_ref, m_ref, l_ref) = refs
    tq = q_ref.shape[1]
    tkp = FX_PAST_TILE
    single = kb_ref.shape[1] == tq
    t_idx = 0 if single else pl.program_id(1)
    tri_l = _tri(tq, True)
    tri_u = _tri(tq, False)

    def init():
        crun_ref[...] = jnp.zeros_like(crun_ref)
        if n_past:
            tri_p = _tri(tkp, True)
            for j in range(n_past // tkp):
                cc_p = crun_ref[...] + _tri_dot(tri_p, lfp_ref[0, j * tkp:(j + 1) * tkp, :])
                ccol_ref[j * tkp:(j + 1) * tkp, :] = cc_p * LOG2E
                crun_ref[...] = cc_p[tkp - 1:tkp, :]
        eye = (lax.broadcasted_iota(jnp.int32, (FX_HEADS, FX_HEADS), 0)
               == lax.broadcasted_iota(jnp.int32, (FX_HEADS, FX_HEADS), 1))
        rrun_ref[...] = jnp.sum(jnp.where(eye, crun_ref[...], 0.0), -1, keepdims=True)

    if single:
        init()
    else:
        pl.when(t_idx == 0)(init)

    lf_col = _log_sigmoid(sm_ref[0, :, 2 * ML_HEADS:N_SMALL])
    lf_row = _log_sigmoid(st_ref[0, 2 * ML_HEADS:N_SMALL, :])
    lf_ref[0] = lf_row
    cc = crun_ref[...] + _tri_dot(tri_l, lf_col)
    cr = rrun_ref[...] + _dot_tri(lf_row, tri_u)
    crun_ref[...] = cc[tq - 1:tq, :]
    rrun_ref[...] = cr[:, tq - 1:tq]
    new0 = n_past if single else pl.multiple_of(n_past + t_idx * tq, tq)
    ccol_ref[pl.ds(new0, tq), :] = cc * LOG2E
    cq_rows = cr * LOG2E

    n_sub = tq // tk
    krow = lax.broadcasted_iota(jnp.int32, (tk, tq), 0)
    qcol = lax.broadcasted_iota(jnp.int32, (tk, tq), 1)
    first_head = lax.broadcasted_iota(jnp.int32, (2 * FX_HD, 1), 0) < FX_HD
    qt = (q_ref[0] * (FX_HD ** -0.5 * LOG2E)).T

    for h in range(FX_HEADS):
        qp = qt[(h // 2) * 2 * FX_HD:(h // 2 + 1) * 2 * FX_HD]
        qm_ref[h] = (jnp.where(first_head, qp, 0.0) if h % 2 == 0 else jnp.where(first_head, 0.0, qp)).astype(BF16)
    yt_ref[...] = jnp.zeros_like(yt_ref)
    m_ref[...] = jnp.full(m_ref.shape, NEG_BIG, F32)
    l_ref[...] = jnp.zeros_like(l_ref)

    def tiles(*key_tiles):
        all_scores = []
        for k_pairs, _, _, _, k_is_transposed in key_tiles:
            qk = _dot_tn if k_is_transposed else _dot
            all_scores.append([qk(k_pairs[h // 2], qm_ref[h]) for h in range(FX_HEADS)])
        for (_, vt_pairs, ck, mask, _), ss in zip(key_tiles, all_scores):
            softmax_and_pv(ss, vt_pairs, ck, mask)

    def softmax_and_pv(ss, vt_pairs, ck, mask):
        ps, alphas = [], []
        for h in range(FX_HEADS):
            m_, l_ = m_ref[h:h + 1, :], l_ref[h:h + 1, :]
            u = ss[h] - ck[:, h:h + 1]
            if mask is not None:
                u = jnp.where(mask, u, NEG_BIG)
            cq = cq_rows[h:h + 1, :]
            mn = jnp.maximum(m_, jnp.max(u, 0, keepdims=True) + cq)
            pe = jnp.exp2(u + (cq - mn))
            a = jnp.exp2(m_ - mn)
            m_ref[h:h + 1, :] = mn
            l_ref[h:h + 1, :] = a * l_ + jnp.sum(pe, 0, keepdims=True)
            ps.append(pe.astype(BF16))
            alphas.append(a)
        for h in range(FX_HEADS):
            p, e = divmod(h, 2)
            hrows = slice(h * FX_HD, (h + 1) * FX_HD)
            yt_ref[hrows, :] = alphas[h] * yt_ref[hrows, :] + _dot(vt_pairs[p][e * FX_HD:(e + 1) * FX_HD, :], ps[h])

    pair_lanes = [slice(p * 2 * FX_HD, (p + 1) * 2 * FX_HD) for p in range(FX_HEADS // 2)]
    def past_tile(j):
        k0 = pl.multiple_of(j * tkp, tkp)
        ks = [kp_ref[0, 0, pp, pl.ds(k0, tkp)].astype(BF16) for pp in pair_lanes]
        vs = [vp_ref[0, 0, pp, pl.ds(k0, tkp)].astype(BF16) for pp in pair_lanes]
        return ks, vs, ccol_ref[pl.ds(k0, tkp), :], None, True

    def new_tile(j, mask=None):
        k0 = j * tk if isinstance(j, int) else pl.multiple_of(j * tk, tk)
        ks = [kb_ref[0, pl.ds(k0, tk), pp] for pp in pair_lanes]
        vs = [vt_ref[0, pp, pl.ds(k0, tk)] for pp in pair_lanes]
        return ks, vs, ccol_ref[pl.ds(n_past + k0, tk), :], mask, False

    if n_past:
        n_pt = n_past // tkp
        lax.fori_loop(0, n_pt // 2, lambda j, c: (tiles(past_tile(2 * j), past_tile(2 * j + 1)), c)[1], 0)
        if n_pt % 2:
            tiles(past_tile(n_pt - 1))

    def diag_tiles(first_sub):
        return [new_tile(first_sub + i, krow + i * tk <= qcol) for i in range(n_sub)]

    if single:
        tiles(*diag_tiles(0))
    else:
        n_full = t_idx * n_sub
        lax.fori_loop(0, n_full // 2, lambda j, c: (tiles(new_tile(2 * j), new_tile(2 * j + 1)), c)[1], 0)

        @pl.when(lax.rem(n_full, 2) == 1)
        def _():
            tiles(new_tile(n_full - 1), *diag_tiles(n_full))

        @pl.when(lax.rem(n_full, 2) == 0)
        def _():
            tiles(*diag_tiles(n_full))
    for h in range(FX_HEADS):
        hrows = slice(h * FX_HD, (h + 1) * FX_HD)
        yt_ref[hrows, :] = yt_ref[hrows, :] / l_ref[h:h + 1, :]
    y_ref[0] = yt_ref[...].T.astype(y_ref.dtype)


def _fox(pa3, st, kb3, vt3, past=None, layer=0):
    b, t, _ = pa3.shape
    tq = min(t, 256)
    tk = min(tq, FX_KEY_TILE)
    nt = t // tq
    n_past = 0 if past is None else past[2].shape[1]
    seq = lambda width, blk: pl.BlockSpec((1, tq, width), lambda i, j: (i, j, blk))
    full = lambda n, width: pl.BlockSpec((1, n, width), lambda i, j: (i, 0, 0))
    in_specs = [seq(BR_W, 0), seq(SMALL_W, N_MAIN // SMALL_W),
                pl.BlockSpec((1, 2 * 8, tq), lambda i, j: (i, 0, j)),
                full(t, BR_W), full(BR_W, t)]
    args = [pa3, pa3, st, kb3, vt3]
    assert n_past % FX_PAST_TILE == 0
    if n_past:
        cache = pl.BlockSpec((1, 1, BR_W, n_past), lambda i, j: (layer, i, 0, 0))
        in_specs += [cache, cache, full(n_past, FX_HEADS)]
        args += list(past)
    return pl.pallas_call(
        functools.partial(_fox_kernel, n_past=n_past, tk=tk),
        grid=(b, nt),
        in_specs=in_specs,
        out_specs=[pl.BlockSpec((1, tq, BR_W), lambda i, j: (i, j, 0)),
                   pl.BlockSpec((1, FX_HEADS, tq), lambda i, j: (i, 0, j))],
        out_shape=[jax.ShapeDtypeStruct((b, t, BR_W), BF16),
                   jax.ShapeDtypeStruct((b, FX_HEADS, t), F32)],
        scratch_shapes=[pltpu.VMEM((n_past + t, FX_HEADS), F32),
                        pltpu.VMEM((1, FX_HEADS), F32), pltpu.VMEM((FX_HEADS, 1), F32),
                        pltpu.VMEM((BR_W, tq), F32), pltpu.VMEM((FX_HEADS, 2 * FX_HD, tq), BF16),
                        pltpu.VMEM((FX_HEADS, tq), F32), pltpu.VMEM((FX_HEADS, tq), F32)],
        compiler_params=_cparams("arbitrary", "arbitrary"),
        name="fox",
    )(*args)


def _hgrn_tile(zf, vi, qv_all, gv_all, first, last, lb_ref, gn_ref, s0_ref, y_ref, s_ref, st_ref):
    tt = zf.shape[0]
    L = HG_CHUNK
    B = HG_BLOCK
    assert L == 4 * B

    @pl.when(first)
    def _():
        for h in range(HG_HEADS):
            st_ref[h] = s0_ref[0, h].T

    tri = _tri(L, True)
    row = lax.broadcasted_iota(jnp.int32, (L, 1), 0)
    rb = lax.broadcasted_iota(jnp.int32, (L, L), 0) // B
    cb = lax.broadcasted_iota(jnp.int32, (L, L), 1) // B
    m_diag = _tri(L, True, B) > 0.5
    m_mid = ((rb == 1) & (cb == 0)) | ((rb == 3) & (cb == 2))
    m_far = (rb >= 2) & (cb < 2)
    lb = lb_ref[...]
    heads = [slice(h * HG_DK, (h + 1) * HG_DK) for h in range(HG_HEADS)]
    ctx = [dict() for _ in range(tt // L)]

    def gates(c):
        k = ctx[c]
        z = zf[c * L:(c + 1) * L, :]
        k['logf'] = jnp.log(jnp.maximum(lb + (1.0 - lb) * _sigmoid(z), TINY))
        k['kk'] = (1.0 - lb) * _sigmoid(-z)
        qv = qv_all[c * L:(c + 1) * L, :]
        k['qq'] = qv * _sigmoid(qv)

    def cumsum(c):
        ctx[c]['b'] = _tri_dot(tri, ctx[c]['logf'])

    def decays(c):
        k = ctx[c]
        b, qq, kk = k['b'], k['qq'], k['kk']
        r1, r2, r3, r4 = (b[j * B - 1:j * B, :] for j in (1, 2, 3, 4))
        start = jnp.where(row < B, 0.0, jnp.where(row < 2 * B, r1, jnp.where(row < 3 * B, r2, r3)))
        mid = jnp.where(row < 2 * B, r1, r3)
        bd = b - start
        ops = dict(
            qd=qq * jnp.exp(bd), kd=kk * jnp.exp(-bd),
            qm=qq * jnp.exp(jnp.minimum(b - mid, 0.0)), km=kk * jnp.exp(jnp.minimum(mid - b, 0.0)),
            qf=qq * jnp.exp(jnp.minimum(b - r2, 0.0)), kf=kk * jnp.exp(jnp.minimum(r2 - b, 0.0)),
            qs=qq * jnp.exp(b), ks=kk * jnp.exp(r4 - b))
        k['ops'] = {name: v.astype(BF16) for name, v in ops.items()}
        k['e4'] = jnp.exp(r4)
        k['vs'] = [vi[c * L:(c + 1) * L, hl].astype(BF16) for hl in heads]

    def scores(c):
        k = ctx[c]
        ops = k['ops']
        k['a_d'] = [_dot_nt(ops['qd'][:, hl], ops['kd'][:, hl]) for hl in heads]
        k['a_m'] = [_dot_nt(ops['qm'][:, hl], ops['km'][:, hl]) for hl in heads]
        k['a_f'] = [_dot_nt(ops['qf'][:, hl], ops['kf'][:, hl]) for hl in heads]

    def masks(c):
        k = ctx[c]
        k['a'] = [(jnp.where(m_diag, k['a_d'][h], 0.0) + jnp.where(m_mid, k['a_m'][h], 0.0)
                   + jnp.where(m_far, k['a_f'][h], 0.0)).astype(BF16) for h in range(HG_HEADS)]

    def outputs(c):
        k = ctx[c]
        ops, vs = k['ops'], k['vs']
        k['s_old'] = [st_ref[h] for h in range(HG_HEADS)]
        k['o'] = [_dot(k['a'][h], vs[h]) + _dot_nt(ops['qs'][:, heads[h]], k['s_old'][h].astype(BF16))
                  for h in range(HG_HEADS)]
        k['ds'] = [_dot_tn(vs[h], ops['ks'][:, hl]) for h, hl in enumerate(heads)]

    def finish_chunk(c):
        k = ctx[c]
        for h, hl in enumerate(heads):
            st_ref[h] = k['e4'][:, hl] * k['s_old'][h] + k['ds'][h]
        for h, hl in enumerate(heads):
            o = k['o'][h]
            rms = lax.rsqrt(jnp.mean(o * o, -1, keepdims=True) + 1e-6)
            gv = gv_all[c * L:(c + 1) * L, hl]
            y_ref[0, c * L:(c + 1) * L, hl] = (o * rms * gn_ref[:, hl] * (gv * _sigmoid(gv))).astype(y_ref.dtype)

    def finish():
        @pl.when(last)
        def _():
            for h in range(HG_HEADS):
                s_ref[0, h] = st_ref[h].T

    stages = (gates, cumsum, decays, scores, masks, outputs, finish_chunk)
    n_chunk, skew = tt // L, 2
    steps = []
    for slot in range(len(stages) + skew * (n_chunk - 1)):
        for c in range(n_chunk):
            si = slot - skew * c
            if 0 <= si < len(stages):
                steps.append(functools.partial(stages[si], c))
    return steps + [finish]


RG_PAD = 8
RG_PIECE = 64


def _rglru_tile(x, g, first, cw_ref, cb_ref, wa_ref, ba_ref, wx_ref, bx_ref, lam_ref,
                h0_ref, buf0_ref, y_ref, h_ref, buf_ref, xp_ref):
    tt = x.shape[0]
    hist = RG_CONV - 1

    @pl.when(first)
    def _():
        h_ref[...] = h0_ref[...]
        xp_ref[RG_PAD - hist:RG_PAD, :] = buf0_ref[0]

    xp_ref[RG_PAD:RG_PAD + tt, :] = x[...]
    n_piece = max(1, tt // RG_PIECE)
    rows = tt // n_piece
    softplus_neg = jax.nn.softplus(-lam_ref[...])
    sub = lax.broadcasted_iota(jnp.int32, (1, SUBLANE, 1), 1)

    ctx = [dict() for _ in range(n_piece)]

    def conv(i):
        r0 = i * rows
        xr = x[r0:r0 + rows]
        u = cb_ref[...] + xr * cw_ref[hist:hist + 1, :]
        for j in range(hist):
            u = u + xp_ref[RG_PAD - hist + j + r0:RG_PAD - hist + j + r0 + rows, :] * cw_ref[j:j + 1, :]
        ctx[i]['u'] = u

    def gate_dots(i):
        ub = ctx[i]['u'].astype(BF16)
        ctx[i]['ra'] = _dot(ub, wa_ref[...])
        ctx[i]['rx'] = _dot(ub, wx_ref[...])

    def scan(i):
        r0 = i * rows
        u = ctx[i]['u']
        r = _sigmoid(ctx[i]['ra'] + ba_ref[...])
        ig = _sigmoid(ctx[i]['rx'] + bx_ref[...])
        log_a = -RG_C * r * softplus_neg
        a = jnp.exp(log_a)
        bt = jnp.sqrt(jnp.maximum(-jnp.tanh(log_a) * (a * a + 1.0), 0.0)) * (ig * u)
        n_grp = rows // SUBLANE
        a3 = a.reshape(n_grp, SUBLANE, RG_W)
        b3 = bt.reshape(n_grp, SUBLANE, RG_W)
        s = 1
        while s < SUBLANE:
            keep = sub >= s
            a_sh = jnp.where(keep, pltpu.roll(a3, s, 1), 1.0)
            b_sh = jnp.where(keep, pltpu.roll(b3, s, 1), 0.0)
            b3 = a3 * b_sh + b3
            a3 = a3 * a_sh
            s *= 2
        h_in = h_ref[0]
        hs = []
        for gi in range(n_grp):
            hs.append(a3[gi] * h_in + b3[gi])
            h_in = hs[-1][SUBLANE - 1:SUBLANE, :]
        h_ref[0] = h_in
        y_ref[0, r0:r0 + rows, :] = (jnp.concatenate(hs, axis=0) * _gelu(g[r0:r0 + rows])).astype(y_ref.dtype)

    def finish():
        new_buf = xp_ref[RG_PAD + tt - hist:RG_PAD + tt, :]
        xp_ref[RG_PAD - hist:RG_PAD, :] = new_buf
        buf_ref[0] = new_buf

    stages = (conv, gate_dots, scan)
    steps = []
    for slot in range(len(stages) + n_piece - 1):
        for i in range(n_piece):
            if 0 <= slot - i < len(stages):
                steps.append(functools.partial(stages[slot - i], i))
    return steps + [finish]


def _merge_kernel(x_ref, y0_ref, y1_ref, y2_ref, y3_ref, wmg_ref, bmg_ref, wbr_ref, wout_ref,
                  g_ref, b_ref, o_ref, *, alpha):
    x = x_ref[...]
    xb = x.astype(BF16)
    mix = None
    for m, y_ref in enumerate((y0_ref, y1_ref, y2_ref, y3_ref)):
        gate = _sigmoid(_dot(xb, wmg_ref[m]) + bmg_ref[m])
        term = gate * _dot(y_ref[...], wbr_ref[m])
        mix = term if mix is None else mix + term
    o_ref[...] = _layer_norm(alpha * x + _dot(mix.astype(BF16), wout_ref[...]), g_ref[...], b_ref[...])


def _merge(x2d, ys, wmg, bmg, wbr, wout, g, bb, alpha):
    m = x2d.shape[0]
    tm = min(m, DENSE_ROWS)
    row = lambda width: pl.BlockSpec((tm, width), lambda i: (i, 0))
    c2 = lambda shape: pl.BlockSpec(shape, lambda i: (0, 0))
    c3 = lambda shape: pl.BlockSpec(shape, lambda i: (0, 0, 0))
    return pl.pallas_call(
        functools.partial(_merge_kernel, alpha=alpha),
        grid=(m // tm,),
        in_specs=[row(D_MODEL)] + [row(BR_W)] * N_BRANCH +
                 [c3((N_BRANCH, D_MODEL, D_MODEL)), c3((N_BRANCH, 1, D_MODEL)),
                  c3((N_BRANCH, BR_W, D_MODEL)), c2((D_MODEL, D_MODEL)),
                  c2((1, D_MODEL)), c2((1, D_MODEL))],
        out_specs=row(D_MODEL),
        out_shape=jax.ShapeDtypeStruct((m, D_MODEL), F32),
        compiler_params=_cparams("arbitrary"),
        name="merge",
    )(x2d, *ys, wmg, bmg, wbr, wout, g, bb)


FF_PAD = 8
FF_COL_BLK = 512


def _ffn_kernel(x_ref, wg_ref, wu_ref, cw_ref, cb_ref, wd_ref, g_ref, b_ref, buf0_ref,
                o_ref, buf_ref, gp_ref, *, alpha):
    tt = x_ref.shape[1]
    hist = FFN_CONV - 1

    @pl.when(pl.program_id(1) == 0)
    def _():
        gp_ref[FF_PAD - hist:FF_PAD, :] = buf0_ref[0]

    x = x_ref[0]
    xb = x.astype(BF16)
    chunks = [slice(c0, c0 + FF_COL_BLK) for c0 in range(0, D_FF, FF_COL_BLK)]

    def in_proj(cols):
        gp = _dot(xb, wg_ref[:, cols])
        gp_ref[FF_PAD:FF_PAD + tt, cols] = gp
        return gp, _dot(xb, wu_ref[:, cols])

    def out_proj(cols, gp, up):
        gc = cb_ref[:, cols] + gp * cw_ref[hist:hist + 1, cols]
        for j in range(hist):
            gc = gc + gp_ref[FF_PAD - hist + j:FF_PAD - hist + j + tt, cols] * cw_ref[j:j + 1, cols]
        return _dot((_gelu(gc) * up).astype(BF16), wd_ref[cols, :])

    acc = None
    pending = in_proj(chunks[0])
    for i, cols in enumerate(chunks):
        nxt = in_proj(chunks[i + 1]) if i + 1 < len(chunks) else None
        part = out_proj(cols, *pending)
        acc = part if acc is None else acc + part
        pending = nxt
    new_buf = gp_ref[FF_PAD + tt - hist:FF_PAD + tt, :]
    gp_ref[FF_PAD - hist:FF_PAD, :] = new_buf
    buf_ref[0] = new_buf
    o_ref[0] = _layer_norm(alpha * x + acc, g_ref[...], b_ref[...])


def _ffn(x3, wg, wu, cw, cb, wd, g, bb, buf0, alpha):
    b, t, _ = x3.shape
    tt = min(t, DENSE_ROWS)
    nt = t // tt
    c2 = lambda shape: pl.BlockSpec(shape, lambda i, j: (0, 0))
    buf_spec = pl.BlockSpec((1, FFN_CONV - 1, D_FF), lambda i, j: (i, 0, 0))
    seq = pl.BlockSpec((1, tt, D_MODEL), lambda i, j: (i, j, 0))
    return pl.pallas_call(
        functools.partial(_ffn_kernel, alpha=alpha),
        grid=(b, nt),
        in_specs=[seq, c2((D_MODEL, D_FF)), c2((D_MODEL, D_FF)), c2((FFN_CONV, D_FF)), c2((1, D_FF)),
                  c2((D_FF, D_MODEL)), c2((1, D_MODEL)), c2((1, D_MODEL)), buf_spec],
        out_specs=[seq, buf_spec],
        out_shape=[jax.ShapeDtypeStruct(x3.shape, F32), jax.ShapeDtypeStruct(buf0.shape, F32)],
        scratch_shapes=[pltpu.VMEM((FF_PAD + tt, D_FF), F32)],
        compiler_params=_cparams("arbitrary", "arbitrary"),
        name="ffn",
    )(x3, wg, wu, cw, cb, wd, g, bb, buf0)


def _prep_layer(l, w_in, b_in, ml_norm_g, hg_norm_g, lbs, rg_conv_w, rg_conv_b, rg_w_a, rg_b_a,
                rg_w_x, rg_b_x, rg_lambda, w_mg, b_mg, w_br, w_out, ln1_g, ln1_b,
                w_ff_gate, w_ff_up, ff_conv_w, ff_conv_b, w_ff_down, ln2_g, ln2_b):
    offs = [int(o) for o in np.concatenate([[0], np.cumsum(PROJ_SIZES)])]
    wl, bl = w_in[l].astype(BF16), b_in[l]
    cols = lambda a, segs: [a[..., offs[s]:offs[s + 1]] for s in segs]
    w_small = jnp.concatenate(cols(wl, SMALL_SEGS), axis=1)
    b_small = jnp.concatenate(cols(bl, SMALL_SEGS))
    w = jnp.concatenate(cols(wl, MAIN_SEGS) + [w_small, jnp.zeros((D_MODEL, SMALL_W - N_SMALL), BF16)]
                        + cols(wl, STASH_SEGS) + cols(wl, KV_SEGS), axis=1)
    bvec = jnp.concatenate(cols(bl, MAIN_SEGS) + [b_small, jnp.zeros((SMALL_W - N_SMALL,), F32)]
                           + cols(bl, STASH_SEGS) + cols(bl, KV_SEGS))[None, :]
    wst = w_small.T
    bst = b_small[:, None]

    def block_diag(wb):
        eye = jnp.eye(RG_BLOCKS, dtype=F32)
        return jnp.einsum('nde,nm->ndme', wb, eye).reshape(RG_W, RG_W).astype(BF16)

    row = lambda v: v[None, :].astype(F32)
    return dict(
        w=w, b=bvec, wst=wst, bst=bst, ml_g=row(ml_norm_g[l]), hg_g=row(hg_norm_g[l]), lb=row(lbs[l]),
        rg_cw=rg_conv_w[l], rg_cb=row(rg_conv_b[l]), rg_wa=block_diag(rg_w_a[l]), rg_ba=row(rg_b_a[l]),
        rg_wx=block_diag(rg_w_x[l]), rg_bx=row(rg_b_x[l]), rg_lam=row(rg_lambda[l]),
        wmg=w_mg[l].astype(BF16), bmg=b_mg[l][:, None, :], wbr=w_br[l].astype(BF16),
        wout=w_out[l].astype(BF16), ln1_g=row(ln1_g[l]), ln1_b=row(ln1_b[l]),
        wg=w_ff_gate[l].astype(BF16), wu=w_ff_up[l].astype(BF16), ff_cw=ff_conv_w[l],
        ff_cb=row(ff_conv_b[l]), wd=w_ff_down[l].astype(BF16), ln2_g=row(ln2_g[l]), ln2_b=row(ln2_b[l]))


def _kv_t_minor(t):
    return t >= LANE


def _trunk_layer(x, p, fox_past, ml_c, ml_n, ml_m, hg_s, rg_h, rg_buf, ff_buf, alpha, layer, depth, kv_bufs):
    b, t, _ = x.shape
    assert t % 64 == 0 and (t <= 256 or t % 256 == 0)
    x2d = x.reshape(b * t, D_MODEL)
    rg = (p['rg_cw'], p['rg_cb'], p['rg_wa'], p['rg_ba'], p['rg_wx'], p['rg_bx'], p['rg_lam'],
          rg_h.reshape(b, 1, RG_W), rg_buf)
    hg = (p['lb'], p['hg_g'], hg_s)
    ml = (ml_c, ml_n, ml_m.reshape(b, 1, ML_HEADS), p['ml_g'])
    (pa3, k32, v32, kb, vt, st, y_hg, hg_s, y_rg, rg_h, rg_buf, y_ml, ml_c, ml_n, ml_m) = _proj(
        x, p['w'], p['b'], p['wst'], p['bst'], hg, rg, ml, layer, depth, _kv_t_minor(t), kv_bufs)
    past = None
    if fox_past is not None:
        k_cache, v_cache, logf_past = fox_past
        to_feature_major = lambda a: jnp.transpose(a, (0, 1, 3, 4, 2)).reshape(depth, b, BR_W, a.shape[2])
        past = (to_feature_major(k_cache), to_feature_major(v_cache), logf_past)
    y_fx, f_log = _fox(pa3, st, kb, vt, past, layer)
    f_log = jnp.swapaxes(f_log, 1, 2)
    ys = [y.reshape(b * t, BR_W) for y in (y_ml, y_fx, y_hg, y_rg)]
    x1 = _merge(x2d, ys, p['wmg'], p['bmg'], p['wbr'], p['wout'], p['ln1_g'], p['ln1_b'], alpha)
    x2, ff_buf = _ffn(x1.reshape(b, t, D_MODEL), p['wg'], p['wu'], p['ff_cw'], p['ff_cb'], p['wd'],
                      p['ln2_g'], p['ln2_b'], ff_buf, alpha)
    state = (f_log, ml_c, ml_n, ml_m.reshape(b, ML_HEADS), hg_s, rg_h.reshape(b, RG_W), rg_buf, ff_buf)
    return x2, (k32, v32), state


def _kv_output(buf, t):
    depth, b = buf.shape[:2]
    if _kv_t_minor(t):
        return jnp.transpose(buf.reshape(depth, b, FX_HEADS, FX_HD, t), (0, 1, 4, 2, 3))
    return buf.reshape(depth, b, t, FX_HEADS, FX_HD)


def kernel(x_prompt, x_sample, cache_fox_k, cache_fox_v, cache_fox_logf, state_mlstm_c, state_mlstm_n,
           state_mlstm_m, state_hgrn_s, state_rglru_h, state_rglru_conv, state_ffn_conv,
           w_in, b_in, ml_norm_g, hg_norm_g, hg_lb_logits, rg_conv_w, rg_conv_b, rg_w_a, rg_b_a,
           rg_w_x, rg_b_x, rg_lambda, w_mg, b_mg, w_br, w_out, ln1_g, ln1_b,
           w_ff_gate, w_ff_up, ff_conv_w, ff_conv_b, w_ff_down, ln2_g, ln2_b):
    depth = w_in.shape[0]
    alpha = (2 * depth) ** 0.25
    pl_soft = jax.nn.softmax(hg_lb_logits.astype(F32), axis=0)
    lbs = jnp.cumsum(pl_soft, axis=0) - pl_soft[0]
    bp = x_prompt.shape[0]
    yp, ys = x_prompt, x_sample
    p_new, s_new = [], []
    kv_p, kv_s = (), ()
    for l in range(depth):
        p = _prep_layer(l, w_in, b_in, ml_norm_g, hg_norm_g, lbs, rg_conv_w, rg_conv_b, rg_w_a, rg_b_a,
                        rg_w_x, rg_b_x, rg_lambda, w_mg, b_mg, w_br, w_out, ln1_g, ln1_b,
                        w_ff_gate, w_ff_up, ff_conv_w, ff_conv_b, w_ff_down, ln2_g, ln2_b)
        z = lambda *shape: jnp.zeros((bp,) + shape, F32)
        yp, kv_p, st_p = _trunk_layer(yp, p, None, z(ML_HEADS, ML_DK, ML_DV), z(ML_HEADS, ML_DK), z(ML_HEADS),
                                      z(HG_HEADS, HG_DK, HG_DK), z(RG_W), z(RG_CONV - 1, RG_W),
                                      z(FFN_CONV - 1, D_FF), alpha, l, depth, kv_p if l else ())
        p_new.append(st_p)
        ys, kv_s, st_s = _trunk_layer(ys, p, (cache_fox_k, cache_fox_v, cache_fox_logf[l]),
                                      state_mlstm_c[l], state_mlstm_n[l], state_mlstm_m[l], state_hgrn_s[l],
                                      state_rglru_h[l], state_rglru_conv[l], state_ffn_conv[l], alpha,
                                      l, depth, kv_s if l else ())
        s_new.append(st_s)
    n_st = len(p_new[0])
    p_out = [jnp.stack([st[j] for st in p_new]) for j in range(n_st)]
    s_out = [jnp.stack([st[j] for st in s_new]) for j in range(n_st)]
    tp, ts = x_prompt.shape[1], x_sample.shape[1]
    return (yp, ys, _kv_output(kv_p[0], tp), _kv_output(kv_p[1], tp), *p_out,
            _kv_output(kv_s[0], ts), _kv_output(kv_s[1], ts), *s_out)
```

```python
import functools

import numpy as np
import jax
import jax.numpy as jnp
from jax import lax
from jax.experimental import pallas as pl
from jax.experimental.pallas import tpu as pltpu

F32 = jnp.float32
BF16 = jnp.bfloat16

D_MODEL = 1024
BR_W = D_MODEL // 2
ML_HEADS = 4
ML_DV = BR_W // ML_HEADS
ML_DK = ML_DV // 2
ML_CHUNK = 128
FX_HD = 64
FX_HEADS = BR_W // FX_HD
FX_PAST_TILE = 256
FX_KEY_TILE = 256
LOG2E = 1.4426950408889634
HG_HEADS = 4
HG_DK = BR_W // HG_HEADS
HG_BLOCK = 16
HG_CHUNK = 64
RG_W = BR_W
RG_BLOCKS = 8
RG_BD = RG_W // RG_BLOCKS
RG_CONV = 4
RG_C = 8.0
D_FF = 2 * D_MODEL
FFN_CONV = 3
N_BRANCH = 4
PROJ_SIZES = (ML_HEADS * ML_DK, ML_HEADS * ML_DK, BR_W, BR_W, ML_HEADS, ML_HEADS,
              BR_W, BR_W, BR_W, FX_HEADS,
              BR_W, BR_W, BR_W, BR_W,
              RG_W, RG_W)
NEG_BIG = -1e30
TINY = 1e-30

LANE = 128
SUBLANE = 8
SMALL_W = LANE
N_SMALL = 2 * ML_HEADS + FX_HEADS
MAIN_SEGS = (6,)
SMALL_SEGS = (4, 5, 9)
STASH_SEGS = (10, 11, 12, 13, 14, 15, 0, 1, 2, 3)
KV_SEGS = (7, 8)
N_MAIN = sum(PROJ_SIZES[i] for i in MAIN_SEGS)
PA_W = N_MAIN + SMALL_W
STASH_COL0 = PA_W
KV_COL0 = STASH_COL0 + sum(PROJ_SIZES[i] for i in STASH_SEGS)
COL_BLK = 512
DENSE_ROWS = 512
VMEM_LIMIT = 56 * 1024 * 1024


def _cparams(*sem):
    return pltpu.CompilerParams(dimension_semantics=sem, vmem_limit_bytes=VMEM_LIMIT)


def _dot(a, b):
    return jnp.dot(a, b, preferred_element_type=F32)


def _dot_nt(a, b):
    return lax.dot_general(a, b, (((1,), (1,)), ((), ())), preferred_element_type=F32)


def _dot_tn(a, b):
    return lax.dot_general(a, b, (((0,), (0,)), ((), ())), preferred_element_type=F32)


def _split3(x):
    hi = x.astype(BF16)
    r1 = x - hi.astype(F32)
    mid = r1.astype(BF16)
    lo = (r1 - mid.astype(F32)).astype(BF16)
    return hi, mid, lo


def _tri_dot(tri, x):
    t = tri.astype(BF16)
    hi, mid, lo = _split3(x)
    return (_dot(t, lo) + _dot(t, mid)) + _dot(t, hi)


def _dot_tri(x, tri):
    t = tri.astype(BF16)
    hi, mid, lo = _split3(x)
    return (_dot(lo, t) + _dot(mid, t)) + _dot(hi, t)


def _tri(n, lower, block=None):
    r = lax.broadcasted_iota(jnp.int32, (n, n), 0)
    c = lax.broadcasted_iota(jnp.int32, (n, n), 1)
    m = (r >= c) if lower else (r <= c)
    if block is not None:
        m = m & ((r // block) == (c // block))
    return m.astype(F32)


def _dot_hi_nt(a, b):
    return lax.dot_general(a, b, (((1,), (1,)), ((), ())), precision=lax.Precision.HIGHEST,
                           preferred_element_type=F32)


_log_sigmoid = jax.nn.log_sigmoid
_sigmoid = jax.nn.sigmoid
_gelu = jax.nn.gelu


def _layer_norm(v, g, b):
    mu = jnp.mean(v, -1, keepdims=True)
    c = v - mu
    var = jnp.mean(c * c, -1, keepdims=True)
    return c * lax.rsqrt(var + 1e-5) * g + b


N_RG_IN = 9
N_HG_IN = 3
N_ML_IN = 4
N_PROJ_IN = 5 + N_HG_IN + N_RG_IN + N_ML_IN
STASH_W = 9 * COL_BLK + SMALL_W


def _proj_kernel(*refs, t_minor, n_alias, nt):
    x_ref, w_ref, b_ref, wst_ref, bst_ref = refs[:5]
    hg_in = refs[5:5 + N_HG_IN]
    rg_in = refs[5 + N_HG_IN:5 + N_HG_IN + N_RG_IN]
    ml_in = refs[5 + N_HG_IN + N_RG_IN:N_PROJ_IN]
    (pa_ref, k32_ref, v32_ref, kb_ref, vb_ref, st_ref,
     yhg_ref, hgs_ref, yrg_ref, rgh_ref, rgbuf_ref, yml_ref, mlc_ref, mln_ref, mlm_ref,
     hgst_ref, xp_ref, mlct_ref, stash_ref, stash_t_ref) = refs[N_PROJ_IN + n_alias:]
    s = pl.program_id(0)
    slot_w = lax.rem(s, 2)
    slot_r = 1 - slot_w
    t_prev = lax.rem(jnp.maximum(s - 1, 0), nt)
    first, last = t_prev == 0, t_prev == nt - 1

    @pl.when(s == 0)
    def _():
        def zero_rows(i, carry):
            stash_ref[1, pl.ds(pl.multiple_of(i * SUBLANE, SUBLANE), SUBLANE), :] = jnp.zeros((SUBLANE, STASH_W), F32)
            return carry
        lax.fori_loop(0, stash_ref.shape[1] // SUBLANE, zero_rows, 0)
        stash_t_ref[1] = jnp.zeros(stash_t_ref.shape[1:], F32)

    xb = x_ref[0].astype(BF16)
    col = lambda c0, width: _dot(xb, w_ref[:, c0:c0 + width]) + b_ref[:, c0:c0 + width]
    stashed = lambda c0, width: stash_ref.at[slot_r, :, c0:c0 + width]
    blk = lambda j: stashed(j * COL_BLK, COL_BLK)
    hg_steps = _hgrn_tile(blk(0), blk(1), blk(2), blk(3), first, last, *hg_in, yhg_ref, hgs_ref, hgst_ref)
    rg_steps = _rglru_tile(blk(4), blk(5), first, *rg_in, yrg_ref, rgh_ref, rgbuf_ref, xp_ref)
    qk0 = 6 * COL_BLK
    ml_steps = _mlstm_tile(stashed(qk0, ML_HEADS * ML_DK), stashed(qk0 + ML_HEADS * ML_DK, ML_HEADS * ML_DK),
                           blk(7), blk(8), stashed(9 * COL_BLK, SMALL_W), stash_t_ref.at[slot_r],
                           first, last, *ml_in, yml_ref, mlc_ref, mln_ref, mlm_ref, mlct_ref)

    def store_stash(j):
        stash_ref[slot_w, :, j * COL_BLK:(j + 1) * COL_BLK] = col(STASH_COL0 + j * COL_BLK, COL_BLK)

    def store_main(c0):
        width = min(COL_BLK, PA_W - c0)
        val = col(c0, width)
        pa_ref[0, :, c0:c0 + width] = val
        if c0 == N_MAIN:
            stash_ref[slot_w, :, 9 * COL_BLK:] = val

    def store_rows():
        st = _dot_nt(wst_ref[...], xb) + bst_ref[...]
        st_ref[0] = st
        stash_t_ref[slot_w] = st

    def store_k():
        k = col(KV_COL0, BR_W)
        kb_ref[0] = k.astype(BF16)
        k32_ref[0, 0] = k.T if t_minor else k

    def store_v():
        v = col(KV_COL0 + BR_W, BR_W)
        vt = v.T
        vb_ref[0] = vt.astype(BF16)
        v32_ref[0, 0] = vt if t_minor else v

    proj_steps = ([functools.partial(store_main, c0) for c0 in range(0, PA_W, COL_BLK)]
                  + [store_rows, store_k, store_v] + [functools.partial(store_stash, j) for j in range(9)])
    tagged = [((i + (-0.5 if steps is proj_steps else 0.5)) / len(steps), steps[i])
              for steps in (hg_steps, ml_steps, rg_steps, proj_steps) for i in range(len(steps))]
    for _, step in sorted(tagged, key=lambda ps: ps[0]):
        step()


def _proj(x3, w, b, wst, bst, hg, rg, ml, layer, depth, t_minor, kv_bufs):
    bsz, t, _ = x3.shape
    tt = min(t, 256)
    nt = t // tt
    n_tiles = bsz * nt
    nw = w.shape[1]
    cur = lambda s: divmod(jnp.minimum(s, n_tiles - 1), nt)
    prev = lambda s: divmod(jnp.maximum(s - 1, 0), nt)
    seq = lambda width: pl.BlockSpec((1, tt, width), lambda s: (*cur(s), 0))
    seq_prev = lambda width: pl.BlockSpec((1, tt, width), lambda s: (*prev(s), 0))
    const = lambda shape: pl.BlockSpec(shape, lambda s: (0, 0))
    sds = lambda width, dt: jax.ShapeDtypeStruct((bsz, t, width), dt)
    vec = const((1, RG_W))
    mat = const((RG_W, RG_W))
    h_spec = pl.BlockSpec((1, 1, RG_W), lambda s: (prev(s)[0], 0, 0))
    buf_spec = pl.BlockSpec((1, RG_CONV - 1, RG_W), lambda s: (prev(s)[0], 0, 0))
    rg_specs = [const((RG_CONV, RG_W)), vec, mat, vec, mat, vec, vec, h_spec, buf_spec]
    s_spec = pl.BlockSpec((1, HG_HEADS, HG_DK, HG_DK), lambda s: (prev(s)[0], 0, 0, 0))
    hg_specs = [vec, vec, s_spec]
    c_spec = pl.BlockSpec((1, ML_HEADS, ML_DK, ML_DV), lambda s: (prev(s)[0], 0, 0, 0))
    n_spec = pl.BlockSpec((1, ML_HEADS, ML_DK), lambda s: (prev(s)[0], 0, 0))
    m_spec = pl.BlockSpec((1, 1, ML_HEADS), lambda s: (prev(s)[0], 0, 0))
    ml_specs = [c_spec, n_spec, m_spec, vec]
    assert len(rg) == N_RG_IN and len(hg) == N_HG_IN and len(ml) == N_ML_IN
    if t_minor:
        kv_spec = pl.BlockSpec((1, 1, BR_W, tt), lambda s: (layer, cur(s)[0], 0, cur(s)[1]))
        kv_sds = jax.ShapeDtypeStruct((depth, bsz, BR_W, t), F32)
    else:
        kv_spec = pl.BlockSpec((1, 1, tt, BR_W), lambda s: (layer, *cur(s), 0))
        kv_sds = jax.ShapeDtypeStruct((depth, bsz, t, BR_W), F32)
    n_alias = len(kv_bufs)
    return pl.pallas_call(
        functools.partial(_proj_kernel, t_minor=t_minor, n_alias=n_alias, nt=nt),
        grid=(n_tiles + 1,),
        in_specs=[seq(D_MODEL), const((D_MODEL, nw)), const((1, nw)),
                  const((2 * 8, D_MODEL)), const((2 * 8, 1))] + hg_specs + rg_specs + ml_specs
                 + [pl.BlockSpec(memory_space=pl.ANY)] * n_alias,
        out_specs=[seq(PA_W), kv_spec, kv_spec, seq(BR_W),
                   pl.BlockSpec((1, BR_W, tt), lambda s: (cur(s)[0], 0, cur(s)[1])),
                   pl.BlockSpec((1, 2 * 8, tt), lambda s: (cur(s)[0], 0, cur(s)[1])),
                   seq_prev(BR_W), s_spec, seq_prev(RG_W), h_spec, buf_spec,
                   seq_prev(BR_W), c_spec, n_spec, m_spec],
        out_shape=[sds(PA_W, F32), kv_sds, kv_sds, sds(BR_W, BF16),
                   jax.ShapeDtypeStruct((bsz, BR_W, t), BF16),
                   jax.ShapeDtypeStruct((bsz, 2 * 8, t), F32),
                   sds(BR_W, BF16), jax.ShapeDtypeStruct((bsz, HG_HEADS, HG_DK, HG_DK), F32),
                   sds(RG_W, BF16), jax.ShapeDtypeStruct((bsz, 1, RG_W), F32),
                   jax.ShapeDtypeStruct((bsz, RG_CONV - 1, RG_W), F32),
                   sds(BR_W, BF16), jax.ShapeDtypeStruct((bsz, ML_HEADS, ML_DK, ML_DV), F32),
                   jax.ShapeDtypeStruct((bsz, ML_HEADS, ML_DK), F32),
                   jax.ShapeDtypeStruct((bsz, 1, ML_HEADS), F32)],
        input_output_aliases={N_PROJ_IN + a: 1 + a for a in range(n_alias)},
        scratch_shapes=[pltpu.VMEM((HG_HEADS, HG_DK, HG_DK), F32), pltpu.VMEM((RG_PAD + tt, RG_W), F32),
                        pltpu.VMEM((ML_HEADS, ML_DV, ML_DK), F32),
                        pltpu.VMEM((2, tt, STASH_W), F32), pltpu.VMEM((2, 2 * 8, tt), F32)],
        compiler_params=_cparams("arbitrary"),
        name="proj",
    )(x3, w, b, wst, bst, *hg, *rg, *ml, *kv_bufs)


def _mlstm_tile(q_ref, k_ref, v_ref, o_ref, sm_ref, st_ref, first, last, c0_ref, n0_ref, m0_ref, g_ref,
                y_ref, c_ref, n_ref, m_ref, ct_ref):
    tt = q_ref.shape[0]
    L = min(tt, ML_CHUNK)
    H = range(ML_HEADS)

    @pl.when(first)
    def _():
        for h in H:
            ct_ref[h] = c0_ref[0, h].T
        n_ref[...] = n0_ref[...]
        m_ref[...] = m0_ref[...]

    tri_l = _tri(L, True)
    tri_u = _tri(L, False)
    causal_t = tri_u > 0.5
    dk = [slice(h * ML_DK, (h + 1) * ML_DK) for h in H]
    dv = [slice(h * ML_DV, (h + 1) * ML_DV) for h in H]
    n_chunk = tt // L
    ctx = [dict() for _ in range(n_chunk)]

    def gates(c):
        k = ctx[c]
        rows = slice(c * L, (c + 1) * L)
        sm = sm_ref[rows, :]
        st = st_ref[:, rows]
        k['i_col'], k['lf_col'] = sm[:, 0:ML_HEADS], _log_sigmoid(sm[:, ML_HEADS:2 * ML_HEADS])
        k['lf_row'] = _log_sigmoid(st[ML_HEADS:2 * ML_HEADS, :])
        k['qf'] = [q_ref[rows, dk[h]] for h in H]
        k['qb'] = [x.astype(BF16) for x in k['qf']]
        k['kf'] = [k_ref[rows, dk[h]] * (ML_DK ** -0.5) for h in H]
        k['vb'] = [v_ref[rows, dv[h]].astype(BF16) for h in H]

    def cumsums(c):
        k = ctx[c]
        k['a_cols'] = k['i_col'] - _tri_dot(tri_l, k['lf_col'])
        k['fcum_row'] = _dot_tri(k['lf_row'], tri_u)
        k['s_t'] = [_dot_nt(k['kf'][h].astype(BF16), k['qb'][h]) for h in H]

    def weights(c):
        k = ctx[c]
        m_all = m_ref[0]
        k['w_t'], k['g'], k['m_t'], k['ks'] = [], [], [], []
        for h in H:
            fr = k['fcum_row'][h:h + 1, :]
            a_col = k['a_cols'][:, h:h + 1]
            d_t = jnp.where(causal_t, fr + a_col, NEG_BIG)
            prev = fr + m_all[:, h:h + 1]
            m_t = jnp.maximum(prev, jnp.max(d_t, 0, keepdims=True))
            k['m_t'].append(m_t)
            k['w_t'].append(jnp.exp(d_t - m_t) * k['s_t'][h])
            k['g'].append(jnp.exp(prev - m_t))
            k['ks'].append(jnp.exp(fr[:, L - 1:L] + a_col - m_t[:, L - 1:L]) * k['kf'][h])
        head = lax.broadcasted_iota(jnp.int32, (1, ML_HEADS), 1)
        m_vec = k['m_t'][0][:, L - 1:L]
        for h in range(1, ML_HEADS):
            m_vec = jnp.where(head == h, k['m_t'][h][:, L - 1:L], m_vec)
        m_ref[0] = m_vec

    def products(c):
        k = ctx[c]
        k['ct'] = [ct_ref[h] for h in H]
        k['nst'] = [n_ref[0, h:h + 1, :] for h in H]
        k['num_t'] = [_dot_tn(k['vb'][h], k['w_t'][h].astype(BF16)) for h in H]
        k['qc_t'] = [_dot_nt(k['ct'][h].astype(BF16), k['qb'][h]) for h in H]
        k['qn'] = [_dot_hi_nt(k['nst'][h], k['qf'][h]) for h in H]
        k['dct'] = [_dot_tn(k['vb'][h], k['ks'][h].astype(BF16)) for h in H]

    def finish_chunk(c):
        k = ctx[c]
        rows = slice(c * L, (c + 1) * L)
        for h in H:
            g_last = k['g'][h][:, L - 1:L]
            ct_ref[h] = g_last * k['ct'][h] + k['dct'][h]
            n_ref[0, h:h + 1, :] = g_last * k['nst'][h] + jnp.sum(k['ks'][h], 0, keepdims=True)
        for h in H:
            g, m_t = k['g'][h], k['m_t'][h]
            den = jnp.sum(k['w_t'][h], 0, keepdims=True) + g * k['qn'][h]
            h_t = (k['num_t'][h] + g * k['qc_t'][h]) / jnp.maximum(jnp.abs(den), jnp.exp(-m_t))
            mu = jnp.mean(h_t, 0, keepdims=True)
            hc = h_t - mu
            var = jnp.mean(hc * hc, 0, keepdims=True)
            hn = (hc * lax.rsqrt(var + 1e-5)).T
            yv = _sigmoid(o_ref[rows, dv[h]]) * (hn * g_ref[:, dv[h]])
            y_ref[0, rows, dv[h]] = yv.astype(y_ref.dtype)

    def finish():
        @pl.when(last)
        def _():
            for h in H:
                c_ref[0, h] = ct_ref[h].T

    stages = (gates, cumsums, weights, products, finish_chunk)
    steps = []
    skew = 2
    for slot in range(len(stages) + skew * (n_chunk - 1)):
        for c in range(n_chunk):
            if 0 <= slot - skew * c < len(stages):
                steps.append(functools.partial(stages[slot - skew * c], c))
    return steps + [finish]


def _fox_kernel(*refs, n_past, tk):
    if n_past:
        (q_ref, sm_ref, st_ref, kb_ref, vt_ref, kp_ref, vp_ref, lfp_ref,
         y_ref, lf_ref, ccol_ref, crun_ref, rrun_ref, yt_ref, qm_ref, m_ref, l_ref) = refs
    else:
        (q_ref, sm_ref, st_ref, kb_ref, vt_ref,
         y_ref, lf_ref, ccol_ref, crun_ref, rrun_ref, yt_ref, qm_ref, m_ref, l_ref) = refs
    tq = q_ref.shape[1]
    tkp = FX_PAST_TILE
    single = kb_ref.shape[1] == tq
    t_idx = 0 if single else pl.program_id(1)
    tri_l = _tri(tq, True)
    tri_u = _tri(tq, False)

    def init():
        crun_ref[...] = jnp.zeros_like(crun_ref)
        if n_past:
            tri_p = _tri(tkp, True)
            for j in range(n_past // tkp):
                cc_p = crun_ref[...] + _tri_dot(tri_p, lfp_ref[0, j * tkp:(j + 1) * tkp, :])
                ccol_ref[j * tkp:(j + 1) * tkp, :] = cc_p * LOG2E
                crun_ref[...] = cc_p[tkp - 1:tkp, :]
        eye = (lax.broadcasted_iota(jnp.int32, (FX_HEADS, FX_HEADS), 0)
               == lax.broadcasted_iota(jnp.int32, (FX_HEADS, FX_HEADS), 1))
        rrun_ref[...] = jnp.sum(jnp.where(eye, crun_ref[...], 0.0), -1, keepdims=True)

    if single:
        init()
    else:
        pl.when(t_idx == 0)(init)

    lf_col = _log_sigmoid(sm_ref[0, :, 2 * ML_HEADS:N_SMALL])
    lf_row = _log_sigmoid(st_ref[0, 2 * ML_HEADS:N_SMALL, :])
    lf_ref[0] = lf_row
    cc = crun_ref[...] + _tri_dot(tri_l, lf_col)
    cr = rrun_ref[...] + _dot_tri(lf_row, tri_u)
    crun_ref[...] = cc[tq - 1:tq, :]
    rrun_ref[...] = cr[:, tq - 1:tq]
    new0 = n_past if single else pl.multiple_of(n_past + t_idx * tq, tq)
    ccol_ref[pl.ds(new0, tq), :] = cc * LOG2E
    cq_rows = cr * LOG2E

    n_sub = tq // tk
    krow = lax.broadcasted_iota(jnp.int32, (tk, tq), 0)
    qcol = lax.broadcasted_iota(jnp.int32, (tk, tq), 1)
    first_head = lax.broadcasted_iota(jnp.int32, (2 * FX_HD, 1), 0) < FX_HD
    qt = (q_ref[0] * (FX_HD ** -0.5 * LOG2E)).T

    for h in range(FX_HEADS):
        qp = qt[(h // 2) * 2 * FX_HD:(h // 2 + 1) * 2 * FX_HD]
        qm_ref[h] = (jnp.where(first_head, qp, 0.0) if h % 2 == 0 else jnp.where(first_head, 0.0, qp)).astype(BF16)
    yt_ref[...] = jnp.zeros_like(yt_ref)
    m_ref[...] = jnp.full(m_ref.shape, NEG_BIG, F32)
    l_ref[...] = jnp.zeros_like(l_ref)

    def tiles(*key_tiles):
        all_scores = []
        for k_pairs, _, _, _, k_is_transposed in key_tiles:
            qk = _dot_tn if k_is_transposed else _dot
            all_scores.append([qk(k_pairs[h // 2], qm_ref[h]) for h in range(FX_HEADS)])
        for (_, vt_pairs, ck, mask, _), ss in zip(key_tiles, all_scores):
            softmax_and_pv(ss, vt_pairs, ck, mask)

    def softmax_and_pv(ss, vt_pairs, ck, mask):
        ps, alphas = [], []
        for h in range(FX_HEADS):
            m_, l_ = m_ref[h:h + 1, :], l_ref[h:h + 1, :]
            u = ss[h] - ck[:, h:h + 1]
            if mask is not None:
                u = jnp.where(mask, u, NEG_BIG)
            cq = cq_rows[h:h + 1, :]
            mn = jnp.maximum(m_, jnp.max(u, 0, keepdims=True) + cq)
            pe = jnp.exp2(u + (cq - mn))
            a = jnp.exp2(m_ - mn)
            m_ref[h:h + 1, :] = mn
            l_ref[h:h + 1, :] = a * l_ + jnp.sum(pe, 0, keepdims=True)
            ps.append(pe.astype(BF16))
            alphas.append(a)
        for h in range(FX_HEADS):
            p, e = divmod(h, 2)
            hrows = slice(h * FX_HD, (h + 1) * FX_HD)
            yt_ref[hrows, :] = alphas[h] * yt_ref[hrows, :] + _dot(vt_pairs[p][e * FX_HD:(e + 1) * FX_HD, :], ps[h])

    pair_lanes = [slice(p * 2 * FX_HD, (p + 1) * 2 * FX_HD) for p in range(FX_HEADS // 2)]
    def past_tile(j):
        k0 = pl.multiple_of(j * tkp, tkp)
        ks = [kp_ref[0, 0, pp, pl.ds(k0, tkp)].astype(BF16) for pp in pair_lanes]
        vs = [vp_ref[0, 0, pp, pl.ds(k0, tkp)].astype(BF16) for pp in pair_lanes]
        return ks, vs, ccol_ref[pl.ds(k0, tkp), :], None, True

    def new_tile(j, mask=None):
        k0 = j * tk if isinstance(j, int) else pl.multiple_of(j * tk, tk)
        ks = [kb_ref[0, pl.ds(k0, tk), pp] for pp in pair_lanes]
        vs = [vt_ref[0, pp, pl.ds(k0, tk)] for pp in pair_lanes]
        return ks, vs, ccol_ref[pl.ds(n_past + k0, tk), :], mask, False

    if n_past:
        n_pt = n_past // tkp
        lax.fori_loop(0, n_pt // 2, lambda j, c: (tiles(past_tile(2 * j), past_tile(2 * j + 1)), c)[1], 0)
        if n_pt % 2:
            tiles(past_tile(n_pt - 1))

    def diag_tiles(first_sub):
        return [new_tile(first_sub + i, krow + i * tk <= qcol) for i in range(n_sub)]

    if single:
        tiles(*diag_tiles(0))
    else:
        n_full = t_idx * n_sub
        lax.fori_loop(0, n_full // 2, lambda j, c: (tiles(new_tile(2 * j), new_tile(2 * j + 1)), c)[1], 0)

        @pl.when(lax.rem(n_full, 2) == 1)
        def _():
            tiles(new_tile(n_full - 1), *diag_tiles(n_full))

        @pl.when(lax.rem(n_full, 2) == 0)
        def _():
            tiles(*diag_tiles(n_full))
    for h in range(FX_HEADS):
        hrows = slice(h * FX_HD, (h + 1) * FX_HD)
        yt_ref[hrows, :] = yt_ref[hrows, :] / l_ref[h:h + 1, :]
    y_ref[0] = yt_ref[...].T.astype(y_ref.dtype)


def _fox(pa3, st, kb3, vt3, past=None, layer=0):
    b, t, _ = pa3.shape
    tq = min(t, 256)
    tk = min(tq, FX_KEY_TILE)
    nt = t // tq
    n_past = 0 if past is None else past[2].shape[1]
    seq = lambda width, blk: pl.BlockSpec((1, tq, width), lambda i, j: (i, j, blk))
    full = lambda n, width: pl.BlockSpec((1, n, width), lambda i, j: (i, 0, 0))
    in_specs = [seq(BR_W, 0), seq(SMALL_W, N_MAIN // SMALL_W),
                pl.BlockSpec((1, 2 * 8, tq), lambda i, j: (i, 0, j)),
                full(t, BR_W), full(BR_W, t)]
    args = [pa3, pa3, st, kb3, vt3]
    assert n_past % FX_PAST_TILE == 0
    if n_past:
        cache = pl.BlockSpec((1, 1, BR_W, n_past), lambda i, j: (layer, i, 0, 0))
        in_specs += [cache, cache, full(n_past, FX_HEADS)]
        args += list(past)
    return pl.pallas_call(
        functools.partial(_fox_kernel, n_past=n_past, tk=tk),
        grid=(b, nt),
        in_specs=in_specs,
        out_specs=[pl.BlockSpec((1, tq, BR_W), lambda i, j: (i, j, 0)),
                   pl.BlockSpec((1, FX_HEADS, tq), lambda i, j: (i, 0, j))],
        out_shape=[jax.ShapeDtypeStruct((b, t, BR_W), BF16),
                   jax.ShapeDtypeStruct((b, FX_HEADS, t), F32)],
        scratch_shapes=[pltpu.VMEM((n_past + t, FX_HEADS), F32),
                        pltpu.VMEM((1, FX_HEADS), F32), pltpu.VMEM((FX_HEADS, 1), F32),
                        pltpu.VMEM((BR_W, tq), F32), pltpu.VMEM((FX_HEADS, 2 * FX_HD, tq), BF16),
                        pltpu.VMEM((FX_HEADS, tq), F32), pltpu.VMEM((FX_HEADS, tq), F32)],
        compiler_params=_cparams("arbitrary", "arbitrary"),
        name="fox",
    )(*args)


def _hgrn_tile(zf, vi, qv_all, gv_all, first, last, lb_ref, gn_ref, s0_ref, y_ref, s_ref, st_ref):
    tt = zf.shape[0]
    L = HG_CHUNK
    B = HG_BLOCK
    assert L == 4 * B

    @pl.when(first)
    def _():
        for h in range(HG_HEADS):
            st_ref[h] = s0_ref[0, h].T

    tri = _tri(L, True)
    row = lax.broadcasted_iota(jnp.int32, (L, 1), 0)
    rb = lax.broadcasted_iota(jnp.int32, (L, L), 0) // B
    cb = lax.broadcasted_iota(jnp.int32, (L, L), 1) // B
    m_diag = _tri(L, True, B) > 0.5
    m_mid = ((rb == 1) & (cb == 0)) | ((rb == 3) & (cb == 2))
    m_far = (rb >= 2) & (cb < 2)
    lb = lb_ref[...]
    heads = [slice(h * HG_DK, (h + 1) * HG_DK) for h in range(HG_HEADS)]
    ctx = [dict() for _ in range(tt // L)]

    def gates(c):
        k = ctx[c]
        z = zf[c * L:(c + 1) * L, :]
        k['logf'] = jnp.log2(jnp.maximum(lb + (1.0 - lb) * _sigmoid(z), TINY))
        k['kk'] = (1.0 - lb) * _sigmoid(-z)
        qv = qv_all[c * L:(c + 1) * L, :]
        k['qq'] = qv * _sigmoid(qv)

    def cumsum(c):
        ctx[c]['b'] = _tri_dot(tri, ctx[c]['logf'])

    def decays(c):
        k = ctx[c]
        b, qq, kk = k['b'], k['qq'], k['kk']
        r1, r2, r3, r4 = (b[j * B - 1:j * B, :] for j in (1, 2, 3, 4))
        start = jnp.where(row < B, 0.0, jnp.where(row < 2 * B, r1, jnp.where(row < 3 * B, r2, r3)))
        mid = jnp.where(row < 2 * B, r1, r3)
        bd = b - start
        e_mid = jnp.exp2(-jnp.abs(b - mid))
        e_far = jnp.exp2(-jnp.abs(b - r2))
        ops = dict(
            qd=qq * jnp.exp2(bd), kd=kk * jnp.exp2(-bd),
            qm=qq * e_mid, km=kk * e_mid, qf=qq * e_far, kf=kk * e_far,
            qs=qq * jnp.exp2(b), ks=kk * jnp.exp2(r4 - b))
        k['ops'] = {name: v.astype(BF16) for name, v in ops.items()}
        k['e4'] = jnp.exp2(r4)
        k['vs'] = [vi[c * L:(c + 1) * L, hl].astype(BF16) for hl in heads]

    def scores(c):
        k = ctx[c]
        ops = k['ops']
        k['a_d'] = [_dot_nt(ops['qd'][:, hl], ops['kd'][:, hl]) for hl in heads]
        k['a_m'] = [_dot_nt(ops['qm'][:, hl], ops['km'][:, hl]) for hl in heads]
        k['a_f'] = [_dot_nt(ops['qf'][:, hl], ops['kf'][:, hl]) for hl in heads]

    def masks(c):
        k = ctx[c]
        k['a'] = [(jnp.where(m_diag, k['a_d'][h], 0.0) + jnp.where(m_mid, k['a_m'][h], 0.0)
                   + jnp.where(m_far, k['a_f'][h], 0.0)).astype(BF16) for h in range(HG_HEADS)]

    def outputs(c):
        k = ctx[c]
        ops, vs = k['ops'], k['vs']
        k['s_old'] = [st_ref[h] for h in range(HG_HEADS)]
        k['o'] = [_dot(k['a'][h], vs[h]) + _dot_nt(ops['qs'][:, heads[h]], k['s_old'][h].astype(BF16))
                  for h in range(HG_HEADS)]
        k['ds'] = [_dot_tn(vs[h], ops['ks'][:, hl]) for h, hl in enumerate(heads)]

    def finish_chunk(c):
        k = ctx[c]
        for h, hl in enumerate(heads):
            st_ref[h] = k['e4'][:, hl] * k['s_old'][h] + k['ds'][h]
        for h, hl in enumerate(heads):
            o = k['o'][h]
            rms = lax.rsqrt(jnp.mean(o * o, -1, keepdims=True) + 1e-6)
            gv = gv_all[c * L:(c + 1) * L, hl]
            y_ref[0, c * L:(c + 1) * L, hl] = (o * rms * gn_ref[:, hl] * (gv * _sigmoid(gv))).astype(y_ref.dtype)

    def finish():
        @pl.when(last)
        def _():
            for h in range(HG_HEADS):
                s_ref[0, h] = st_ref[h].T

    stages = (gates, cumsum, decays, scores, masks, outputs, finish_chunk)
    n_chunk, skew = tt // L, 2
    steps = []
    for slot in range(len(stages) + skew * (n_chunk - 1)):
        for c in range(n_chunk):
            si = slot - skew * c
            if 0 <= si < len(stages):
                steps.append(functools.partial(stages[si], c))
    return steps + [finish]


RG_PAD = 8
RG_PIECE = 64


def _rglru_tile(x, g, first, cw_ref, cb_ref, wa_ref, ba_ref, wx_ref, bx_ref, lam_ref,
                h0_ref, buf0_ref, y_ref, h_ref, buf_ref, xp_ref):
    tt = x.shape[0]
    hist = RG_CONV - 1

    @pl.when(first)
    def _():
        h_ref[...] = h0_ref[...]
        xp_ref[RG_PAD - hist:RG_PAD, :] = buf0_ref[0]

    xp_ref[RG_PAD:RG_PAD + tt, :] = x[...]
    n_piece = max(1, tt // RG_PIECE)
    rows = tt // n_piece
    softplus_neg = jax.nn.softplus(-lam_ref[...])
    sub = lax.broadcasted_iota(jnp.int32, (1, SUBLANE, 1), 1)

    ctx = [dict() for _ in range(n_piece)]

    def conv(i):
        r0 = i * rows
        xr = x[r0:r0 + rows]
        u = cb_ref[...] + xr * cw_ref[hist:hist + 1, :]
        for j in range(hist):
            u = u + xp_ref[RG_PAD - hist + j + r0:RG_PAD - hist + j + r0 + rows, :] * cw_ref[j:j + 1, :]
        ctx[i]['u'] = u

    def gate_dots(i):
        ub = ctx[i]['u'].astype(BF16)
        ctx[i]['ra'] = _dot(ub, wa_ref[...])
        ctx[i]['rx'] = _dot(ub, wx_ref[...])

    def scan(i):
        r0 = i * rows
        u = ctx[i]['u']
        r = _sigmoid(ctx[i]['ra'] + ba_ref[...])
        ig = _sigmoid(ctx[i]['rx'] + bx_ref[...])
        log_a = -RG_C * r * softplus_neg
        a = jnp.exp(log_a)
        bt = jnp.sqrt(jnp.maximum(-jnp.tanh(log_a) * (a * a + 1.0), 0.0)) * (ig * u)
        n_grp = rows // SUBLANE
        a3 = a.reshape(n_grp, SUBLANE, RG_W)
        b3 = bt.reshape(n_grp, SUBLANE, RG_W)
        s = 1
        while s < SUBLANE:
            keep = sub >= s
            a_sh = jnp.where(keep, pltpu.roll(a3, s, 1), 1.0)
            b_sh = jnp.where(keep, pltpu.roll(b3, s, 1), 0.0)
            b3 = a3 * b_sh + b3
            a3 = a3 * a_sh
            s *= 2
        h_in = h_ref[0]
        hs = []
        for gi in range(n_grp):
            hs.append(a3[gi] * h_in + b3[gi])
            h_in = hs[-1][SUBLANE - 1:SUBLANE, :]
        h_ref[0] = h_in
        y_ref[0, r0:r0 + rows, :] = (jnp.concatenate(hs, axis=0) * _gelu(g[r0:r0 + rows])).astype(y_ref.dtype)

    def finish():
        new_buf = xp_ref[RG_PAD + tt - hist:RG_PAD + tt, :]
        xp_ref[RG_PAD - hist:RG_PAD, :] = new_buf
        buf_ref[0] = new_buf

    stages = (conv, gate_dots, scan)
    steps = []
    for slot in range(len(stages) + n_piece - 1):
        for i in range(n_piece):
            if 0 <= slot - i < len(stages):
                steps.append(functools.partial(stages[slot - i], i))
    return steps + [finish]


def _merge_kernel(x_ref, y0_ref, y1_ref, y2_ref, y3_ref, wmg_ref, bmg_ref, wbr_ref, wout_ref,
                  g_ref, b_ref, o_ref, *, alpha):
    tm = x_ref.shape[0]
    n_part = 2 if tm % 16 == 0 else 1
    parts = [slice(i * tm // n_part, (i + 1) * tm // n_part) for i in range(n_part)]

    def branches(rows):
        xb = x_ref[rows, :].astype(BF16)
        mix = None
        for m, y_ref in enumerate((y0_ref, y1_ref, y2_ref, y3_ref)):
            gate = _sigmoid(_dot(xb, wmg_ref[m]) + bmg_ref[m])
            term = gate * _dot(y_ref[rows, :], wbr_ref[m])
            mix = term if mix is None else mix + term
        return mix

    mixes = [branches(rows) for rows in parts]
    for rows, mix in zip(parts, mixes):
        o_ref[rows, :] = _layer_norm(alpha * x_ref[rows, :] + _dot(mix.astype(BF16), wout_ref[...]),
                                     g_ref[...], b_ref[...])


def _merge(x2d, ys, wmg, bmg, wbr, wout, g, bb, alpha):
    m = x2d.shape[0]
    tm = min(m, DENSE_ROWS)
    row = lambda width: pl.BlockSpec((tm, width), lambda i: (i, 0))
    c2 = lambda shape: pl.BlockSpec(shape, lambda i: (0, 0))
    c3 = lambda shape: pl.BlockSpec(shape, lambda i: (0, 0, 0))
    return pl.pallas_call(
        functools.partial(_merge_kernel, alpha=alpha),
        grid=(m // tm,),
        in_specs=[row(D_MODEL)] + [row(BR_W)] * N_BRANCH +
                 [c3((N_BRANCH, D_MODEL, D_MODEL)), c3((N_BRANCH, 1, D_MODEL)),
                  c3((N_BRANCH, BR_W, D_MODEL)), c2((D_MODEL, D_MODEL)),
                  c2((1, D_MODEL)), c2((1, D_MODEL))],
        out_specs=row(D_MODEL),
        out_shape=jax.ShapeDtypeStruct((m, D_MODEL), F32),
        compiler_params=_cparams("arbitrary"),
        name="merge",
    )(x2d, *ys, wmg, bmg, wbr, wout, g, bb)


FF_PAD = 8
FF_COL_BLK = 512
FF_AHEAD = 2


def _ffn_kernel(x_ref, wg_ref, wu_ref, cw_ref, cb_ref, wd_ref, g_ref, b_ref, buf0_ref,
                o_ref, buf_ref, gp_ref, *, alpha):
    tt = x_ref.shape[1]
    hist = FFN_CONV - 1

    @pl.when(pl.program_id(1) == 0)
    def _():
        gp_ref[FF_PAD - hist:FF_PAD, :] = buf0_ref[0]

    n_part = 2 if tt % 16 == 0 else 1
    parts = [(i * tt // n_part, tt // n_part) for i in range(n_part)]
    xbs = [x_ref[0, r0:r0 + n, :].astype(BF16) for r0, n in parts]
    chunks = [slice(c0, c0 + FF_COL_BLK) for c0 in range(0, D_FF, FF_COL_BLK)]
    units = [(p, cols) for cols in chunks for p in range(n_part)]

    def in_proj(p, cols):
        r0, n = parts[p]
        gp = _dot(xbs[p], wg_ref[:, cols])
        gp_ref[FF_PAD + r0:FF_PAD + r0 + n, cols] = gp
        return gp, _dot(xbs[p], wu_ref[:, cols])

    def out_proj(p, cols, gp, up):
        r0, n = parts[p]
        gc = cb_ref[:, cols] + gp * cw_ref[hist:hist + 1, cols]
        for j in range(hist):
            gc = gc + gp_ref[FF_PAD - hist + j + r0:FF_PAD - hist + j + r0 + n, cols] * cw_ref[j:j + 1, cols]
        return _dot((_gelu(gc) * up).astype(BF16), wd_ref[cols, :])

    acc = [None] * n_part
    ahead = min(FF_AHEAD, len(units))
    pending = [in_proj(*u) for u in units[:ahead]]
    for i, (p, cols) in enumerate(units):
        if i + ahead < len(units):
            pending.append(in_proj(*units[i + ahead]))
        part = out_proj(p, cols, *pending.pop(0))
        acc[p] = part if acc[p] is None else acc[p] + part
    for p, (r0, n) in enumerate(parts):
        o_ref[0, r0:r0 + n, :] = _layer_norm(alpha * x_ref[0, r0:r0 + n, :] + acc[p], g_ref[...], b_ref[...])
    new_buf = gp_ref[FF_PAD + tt - hist:FF_PAD + tt, :]
    gp_ref[FF_PAD - hist:FF_PAD, :] = new_buf
    buf_ref[0] = new_buf


def _ffn(x3, wg, wu, cw, cb, wd, g, bb, buf0, alpha):
    b, t, _ = x3.shape
    tt = min(t, DENSE_ROWS)
    nt = t // tt
    c2 = lambda shape: pl.BlockSpec(shape, lambda i, j: (0, 0))
    buf_spec = pl.BlockSpec((1, FFN_CONV - 1, D_FF), lambda i, j: (i, 0, 0))
    seq = pl.BlockSpec((1, tt, D_MODEL), lambda i, j: (i, j, 0))
    return pl.pallas_call(
        functools.partial(_ffn_kernel, alpha=alpha),
        grid=(b, nt),
        in_specs=[seq, c2((D_MODEL, D_FF)), c2((D_MODEL, D_FF)), c2((FFN_CONV, D_FF)), c2((1, D_FF)),
                  c2((D_FF, D_MODEL)), c2((1, D_MODEL)), c2((1, D_MODEL)), buf_spec],
        out_specs=[seq, buf_spec],
        out_shape=[jax.ShapeDtypeStruct(x3.shape, F32), jax.ShapeDtypeStruct(buf0.shape, F32)],
        scratch_shapes=[pltpu.VMEM((FF_PAD + tt, D_FF), F32)],
        compiler_params=_cparams("arbitrary", "arbitrary"),
        name="ffn",
    )(x3, wg, wu, cw, cb, wd, g, bb, buf0)


def _prep_layer(l, w_in, b_in, ml_norm_g, hg_norm_g, lbs, rg_conv_w, rg_conv_b, rg_w_a, rg_b_a,
                rg_w_x, rg_b_x, rg_lambda, w_mg, b_mg, w_br, w_out, ln1_g, ln1_b,
                w_ff_gate, w_ff_up, ff_conv_w, ff_conv_b, w_ff_down, ln2_g, ln2_b):
    offs = [int(o) for o in np.concatenate([[0], np.cumsum(PROJ_SIZES)])]
    wl, bl = w_in[l].astype(BF16), b_in[l]
    cols = lambda a, segs: [a[..., offs[s]:offs[s + 1]] for s in segs]
    w_small = jnp.concatenate(cols(wl, SMALL_SEGS), axis=1)
    b_small = jnp.concatenate(cols(bl, SMALL_SEGS))
    w = jnp.concatenate(cols(wl, MAIN_SEGS) + [w_small, jnp.zeros((D_MODEL, SMALL_W - N_SMALL), BF16)]
                        + cols(wl, STASH_SEGS) + cols(wl, KV_SEGS), axis=1)
    bvec = jnp.concatenate(cols(bl, MAIN_SEGS) + [b_small, jnp.zeros((SMALL_W - N_SMALL,), F32)]
                           + cols(bl, STASH_SEGS) + cols(bl, KV_SEGS))[None, :]
    wst = w_small.T
    bst = b_small[:, None]

    def block_diag(wb):
        eye = jnp.eye(RG_BLOCKS, dtype=F32)
        return jnp.einsum('nde,nm->ndme', wb, eye).reshape(RG_W, RG_W).astype(BF16)

    row = lambda v: v[None, :].astype(F32)
    return dict(
        w=w, b=bvec, wst=wst, bst=bst, ml_g=row(ml_norm_g[l]), hg_g=row(hg_norm_g[l]), lb=row(lbs[l]),
        rg_cw=rg_conv_w[l], rg_cb=row(rg_conv_b[l]), rg_wa=block_diag(rg_w_a[l]), rg_ba=row(rg_b_a[l]),
        rg_wx=block_diag(rg_w_x[l]), rg_bx=row(rg_b_x[l]), rg_lam=row(rg_lambda[l]),
        wmg=w_mg[l].astype(BF16), bmg=b_mg[l][:, None, :], wbr=w_br[l].astype(BF16),
        wout=w_out[l].astype(BF16), ln1_g=row(ln1_g[l]), ln1_b=row(ln1_b[l]),
        wg=w_ff_gate[l].astype(BF16), wu=w_ff_up[l].astype(BF16), ff_cw=ff_conv_w[l],
        ff_cb=row(ff_conv_b[l]), wd=w_ff_down[l].astype(BF16), ln2_g=row(ln2_g[l]), ln2_b=row(ln2_b[l]))


def _kv_t_minor(t):
    return t >= LANE


def _trunk_layer(x, p, fox_past, ml_c, ml_n, ml_m, hg_s, rg_h, rg_buf, ff_buf, alpha, layer, depth, kv_bufs):
    b, t, _ = x.shape
    assert t % 64 == 0 and (t <= 256 or t % 256 == 0)
    x2d = x.reshape(b * t, D_MODEL)
    rg = (p['rg_cw'], p['rg_cb'], p['rg_wa'], p['rg_ba'], p['rg_wx'], p['rg_bx'], p['rg_lam'],
          rg_h.reshape(b, 1, RG_W), rg_buf)
    hg = (p['lb'], p['hg_g'], hg_s)
    ml = (ml_c, ml_n, ml_m.reshape(b, 1, ML_HEADS), p['ml_g'])
    (pa3, k32, v32, kb, vt, st, y_hg, hg_s, y_rg, rg_h, rg_buf, y_ml, ml_c, ml_n, ml_m) = _proj(
        x, p['w'], p['b'], p['wst'], p['bst'], hg, rg, ml, layer, depth, _kv_t_minor(t), kv_bufs)
    past = None
    if fox_past is not None:
        k_cache, v_cache, logf_past = fox_past
        to_feature_major = lambda a: jnp.transpose(a, (0, 1, 3, 4, 2)).reshape(depth, b, BR_W, a.shape[2])
        past = (to_feature_major(k_cache), to_feature_major(v_cache), logf_past)
    y_fx, f_log = _fox(pa3, st, kb, vt, past, layer)
    f_log = jnp.swapaxes(f_log, 1, 2)
    ys = [y.reshape(b * t, BR_W) for y in (y_ml, y_fx, y_hg, y_rg)]
    x1 = _merge(x2d, ys, p['wmg'], p['bmg'], p['wbr'], p['wout'], p['ln1_g'], p['ln1_b'], alpha)
    x2, ff_buf = _ffn(x1.reshape(b, t, D_MODEL), p['wg'], p['wu'], p['ff_cw'], p['ff_cb'], p['wd'],
                      p['ln2_g'], p['ln2_b'], ff_buf, alpha)
    state = (f_log, ml_c, ml_n, ml_m.reshape(b, ML_HEADS), hg_s, rg_h.reshape(b, RG_W), rg_buf, ff_buf)
    return x2, (k32, v32), state


def _kv_output(buf, t):
    depth, b = buf.shape[:2]
    if _kv_t_minor(t):
        return jnp.transpose(buf.reshape(depth, b, FX_HEADS, FX_HD, t), (0, 1, 4, 2, 3))
    return buf.reshape(depth, b, t, FX_HEADS, FX_HD)


def kernel(x_prompt, x_sample, cache_fox_k, cache_fox_v, cache_fox_logf, state_mlstm_c, state_mlstm_n,
           state_mlstm_m, state_hgrn_s, state_rglru_h, state_rglru_conv, state_ffn_conv,
           w_in, b_in, ml_norm_g, hg_norm_g, hg_lb_logits, rg_conv_w, rg_conv_b, rg_w_a, rg_b_a,
           rg_w_x, rg_b_x, rg_lambda, w_mg, b_mg, w_br, w_out, ln1_g, ln1_b,
           w_ff_gate, w_ff_up, ff_conv_w, ff_conv_b, w_ff_down, ln2_g, ln2_b):
    depth = w_in.shape[0]
    alpha = (2 * depth) ** 0.25
    pl_soft = jax.nn.softmax(hg_lb_logits.astype(F32), axis=0)
    lbs = jnp.cumsum(pl_soft, axis=0) - pl_soft[0]
    bp = x_prompt.shape[0]
    yp, ys = x_prompt, x_sample
    p_new, s_new = [], []
    kv_p, kv_s = (), ()
    for l in range(depth):
        p = _prep_layer(l, w_in, b_in, ml_norm_g, hg_norm_g, lbs, rg_conv_w, rg_conv_b, rg_w_a, rg_b_a,
                        rg_w_x, rg_b_x, rg_lambda, w_mg, b_mg, w_br, w_out, ln1_g, ln1_b,
                        w_ff_gate, w_ff_up, ff_conv_w, ff_conv_b, w_ff_down, ln2_g, ln2_b)
        z = lambda *shape: jnp.zeros((bp,) + shape, F32)
        yp, kv_p, st_p = _trunk_layer(yp, p, None, z(ML_HEADS, ML_DK, ML_DV), z(ML_HEADS, ML_DK), z(ML_HEADS),
                                      z(HG_HEADS, HG_DK, HG_DK), z(RG_W), z(RG_CONV - 1, RG_W),
                                      z(FFN_CONV - 1, D_FF), alpha, l, depth, kv_p if l else ())
        p_new.append(st_p)
        ys, kv_s, st_s = _trunk_layer(ys, p, (cache_fox_k, cache_fox_v, cache_fox_logf[l]),
                                      state_mlstm_c[l], state_mlstm_n[l], state_mlstm_m[l], state_hgrn_s[l],
                                      state_rglru_h[l], state_rglru_conv[l], state_ffn_conv[l], alpha,
                                      l, depth, kv_s if l else ())
        s_new.append(st_s)
    n_st = len(p_new[0])
    p_out = [jnp.stack([st[j] for st in p_new]) for j in range(n_st)]
    s_out = [jnp.stack([st[j] for st in s_new]) for j in range(n_st)]
    tp, ts = x_prompt.shape[1], x_sample.shape[1]
    return (yp, ys, _kv_output(kv_p[0], tp), _kv_output(kv_p[1], tp), *p_out,
            _kv_output(kv_s[0], ts), _kv_output(kv_s[1], ts), *s_out)
```

```python
import functools

import numpy as np
import jax
import jax.numpy as jnp
from jax import lax
from jax.experimental import pallas as pl
from jax.experimental.pallas import tpu as pltpu

F32 = jnp.float32
BF16 = jnp.bfloat16

D_MODEL = 1024
BR_W = D_MODEL // 2
ML_HEADS = 4
ML_DV = BR_W // ML_HEADS
ML_DK = ML_DV // 2
ML_CHUNK = 128
FX_HD = 64
FX_HEADS = BR_W // FX_HD
FX_PAST_TILE = 256
FX_KEY_TILE = 256
LOG2E = 1.4426950408889634
HG_HEADS = 4
HG_DK = BR_W // HG_HEADS
HG_BLOCK = 16
HG_CHUNK = 64
RG_W = BR_W
RG_BLOCKS = 8
RG_BD = RG_W // RG_BLOCKS
RG_CONV = 4
RG_C = 8.0
D_FF = 2 * D_MODEL
FFN_CONV = 3
N_BRANCH = 4
PROJ_SIZES = (ML_HEADS * ML_DK, ML_HEADS * ML_DK, BR_W, BR_W, ML_HEADS, ML_HEADS,
              BR_W, BR_W, BR_W, FX_HEADS,
              BR_W, BR_W, BR_W, BR_W,
              RG_W, RG_W)
NEG_BIG = -1e30
TINY = 1e-30

LANE = 128
SUBLANE = 8
SMALL_W = LANE
N_SMALL = 2 * ML_HEADS + FX_HEADS
MAIN_SEGS = (6,)
SMALL_SEGS = (4, 5, 9)
STASH_SEGS = (10, 11, 12, 13, 14, 15, 0, 1, 2, 3)
KV_SEGS = (7, 8)
N_MAIN = sum(PROJ_SIZES[i] for i in MAIN_SEGS)
PA_W = N_MAIN + SMALL_W
STASH_COL0 = PA_W
KV_COL0 = STASH_COL0 + sum(PROJ_SIZES[i] for i in STASH_SEGS)
COL_BLK = 512
DENSE_ROWS = 512
VMEM_LIMIT = 56 * 1024 * 1024


def _cparams(*sem):
    return pltpu.CompilerParams(dimension_semantics=sem, vmem_limit_bytes=VMEM_LIMIT)


def _dot(a, b):
    return jnp.dot(a, b, preferred_element_type=F32)


def _dot_nt(a, b):
    return lax.dot_general(a, b, (((1,), (1,)), ((), ())), preferred_element_type=F32)


def _dot_tn(a, b):
    return lax.dot_general(a, b, (((0,), (0,)), ((), ())), preferred_element_type=F32)


def _split3(x):
    hi = x.astype(BF16)
    r1 = x - hi.astype(F32)
    mid = r1.astype(BF16)
    lo = (r1 - mid.astype(F32)).astype(BF16)
    return hi, mid, lo


def _tri_dot(tri, x):
    t = tri.astype(BF16)
    hi, mid, lo = _split3(x)
    return (_dot(t, lo) + _dot(t, mid)) + _dot(t, hi)


def _dot_tri(x, tri):
    t = tri.astype(BF16)
    hi, mid, lo = _split3(x)
    return (_dot(lo, t) + _dot(mid, t)) + _dot(hi, t)


def _tri(n, lower, block=None):
    r = lax.broadcasted_iota(jnp.int32, (n, n), 0)
    c = lax.broadcasted_iota(jnp.int32, (n, n), 1)
    m = (r >= c) if lower else (r <= c)
    if block is not None:
        m = m & ((r // block) == (c // block))
    return m.astype(F32)


def _dot_hi_nt(a, b):
    return lax.dot_general(a, b, (((1,), (1,)), ((), ())), precision=lax.Precision.HIGHEST,
                           preferred_element_type=F32)


_log_sigmoid = jax.nn.log_sigmoid
_sigmoid = jax.nn.sigmoid
_gelu = jax.nn.gelu


def _layer_norm(v, g, b):
    mu = jnp.mean(v, -1, keepdims=True)
    c = v - mu
    var = jnp.mean(c * c, -1, keepdims=True)
    return c * lax.rsqrt(var + 1e-5) * g + b


N_RG_IN = 9
N_HG_IN = 3
N_ML_IN = 4
N_PROJ_IN = 5 + N_HG_IN + N_RG_IN + N_ML_IN
STASH_W = 9 * COL_BLK + SMALL_W


def _proj_kernel(*refs, t_minor, n_alias, nt):
    x_ref, w_ref, b_ref, wst_ref, bst_ref = refs[:5]
    hg_in = refs[5:5 + N_HG_IN]
    rg_in = refs[5 + N_HG_IN:5 + N_HG_IN + N_RG_IN]
    ml_in = refs[5 + N_HG_IN + N_RG_IN:N_PROJ_IN]
    (pa_ref, k32_ref, v32_ref, kb_ref, vb_ref, st_ref,
     yhg_ref, hgs_ref, yrg_ref, rgh_ref, rgbuf_ref, yml_ref, mlc_ref, mln_ref, mlm_ref,
     hgst_ref, xp_ref, mlct_ref, stash_ref, stash_t_ref) = refs[N_PROJ_IN + n_alias:]
    s = pl.program_id(0)
    slot_w = lax.rem(s, 2)
    slot_r = 1 - slot_w
    t_prev = lax.rem(jnp.maximum(s - 1, 0), nt)
    first, last = t_prev == 0, t_prev == nt - 1

    @pl.when(s == 0)
    def _():
        def zero_rows(i, carry):
            stash_ref[1, pl.ds(pl.multiple_of(i * SUBLANE, SUBLANE), SUBLANE), :] = jnp.zeros((SUBLANE, STASH_W), F32)
            return carry
        lax.fori_loop(0, stash_ref.shape[1] // SUBLANE, zero_rows, 0)
        stash_t_ref[1] = jnp.zeros(stash_t_ref.shape[1:], F32)

    xb = x_ref[0].astype(BF16)
    col = lambda c0, width: _dot(xb, w_ref[:, c0:c0 + width]) + b_ref[:, c0:c0 + width]
    stashed = lambda c0, width: stash_ref.at[slot_r, :, c0:c0 + width]
    blk = lambda j: stashed(j * COL_BLK, COL_BLK)
    hg_steps = _hgrn_tile(blk(0), blk(1), blk(2), blk(3), first, last, *hg_in, yhg_ref, hgs_ref, hgst_ref)
    rg_steps = _rglru_tile(blk(4), blk(5), first, *rg_in, yrg_ref, rgh_ref, rgbuf_ref, xp_ref)
    qk0 = 6 * COL_BLK
    ml_steps = _mlstm_tile(stashed(qk0, ML_HEADS * ML_DK), stashed(qk0 + ML_HEADS * ML_DK, ML_HEADS * ML_DK),
                           blk(7), blk(8), stashed(9 * COL_BLK, SMALL_W), stash_t_ref.at[slot_r],
                           first, last, *ml_in, yml_ref, mlc_ref, mln_ref, mlm_ref, mlct_ref)

    def store_stash(j):
        stash_ref[slot_w, :, j * COL_BLK:(j + 1) * COL_BLK] = col(STASH_COL0 + j * COL_BLK, COL_BLK)

    def store_main(c0):
        width = min(COL_BLK, PA_W - c0)
        val = col(c0, width)
        pa_ref[0, :, c0:c0 + width] = val
        if c0 == N_MAIN:
            stash_ref[slot_w, :, 9 * COL_BLK:] = val

    def store_rows():
        st = _dot_nt(wst_ref[...], xb) + bst_ref[...]
        st_ref[0] = st
        stash_t_ref[slot_w] = st

    def store_k():
        k = col(KV_COL0, BR_W)
        kb_ref[0] = k.astype(BF16)
        k32_ref[0, 0] = k.T if t_minor else k

    def store_v():
        v = col(KV_COL0 + BR_W, BR_W)
        vt = v.T
        vb_ref[0] = vt.astype(BF16)
        v32_ref[0, 0] = vt if t_minor else v

    proj_steps = ([functools.partial(store_main, c0) for c0 in range(0, PA_W, COL_BLK)]
                  + [store_rows, store_k, store_v] + [functools.partial(store_stash, j) for j in range(9)])
    tagged = [((i + (-0.5 if steps is proj_steps else 0.5)) / len(steps), steps[i])
              for steps in (hg_steps, ml_steps, rg_steps, proj_steps) for i in range(len(steps))]
    for _, step in sorted(tagged, key=lambda ps: ps[0]):
        step()


def _proj(x3, w, b, wst, bst, hg, rg, ml, layer, depth, t_minor, kv_bufs):
    bsz, t, _ = x3.shape
    tt = min(t, 256)
    nt = t // tt
    n_tiles = bsz * nt
    nw = w.shape[1]
    cur = lambda s: divmod(jnp.minimum(s, n_tiles - 1), nt)
    prev = lambda s: divmod(jnp.maximum(s - 1, 0), nt)
    seq = lambda width: pl.BlockSpec((1, tt, width), lambda s: (*cur(s), 0))
    seq_prev = lambda width: pl.BlockSpec((1, tt, width), lambda s: (*prev(s), 0))
    const = lambda shape: pl.BlockSpec(shape, lambda s: (0, 0))
    sds = lambda width, dt: jax.ShapeDtypeStruct((bsz, t, width), dt)
    vec = const((1, RG_W))
    mat = const((RG_W, RG_W))
    h_spec = pl.BlockSpec((1, 1, RG_W), lambda s: (prev(s)[0], 0, 0))
    buf_spec = pl.BlockSpec((1, RG_CONV - 1, RG_W), lambda s: (prev(s)[0], 0, 0))
    rg_specs = [const((RG_CONV, RG_W)), vec, mat, vec, mat, vec, vec, h_spec, buf_spec]
    s_spec = pl.BlockSpec((1, HG_HEADS, HG_DK, HG_DK), lambda s: (prev(s)[0], 0, 0, 0))
    hg_specs = [vec, vec, s_spec]
    c_spec = pl.BlockSpec((1, ML_HEADS, ML_DK, ML_DV), lambda s: (prev(s)[0], 0, 0, 0))
    n_spec = pl.BlockSpec((1, ML_HEADS, ML_DK), lambda s: (prev(s)[0], 0, 0))
    m_spec = pl.BlockSpec((1, 1, ML_HEADS), lambda s: (prev(s)[0], 0, 0))
    ml_specs = [c_spec, n_spec, m_spec, vec]
    assert len(rg) == N_RG_IN and len(hg) == N_HG_IN and len(ml) == N_ML_IN
    if t_minor:
        kv_spec = pl.BlockSpec((1, 1, BR_W, tt), lambda s: (layer, cur(s)[0], 0, cur(s)[1]))
        kv_sds = jax.ShapeDtypeStruct((depth, bsz, BR_W, t), F32)
    else:
        kv_spec = pl.BlockSpec((1, 1, tt, BR_W), lambda s: (layer, *cur(s), 0))
        kv_sds = jax.ShapeDtypeStruct((depth, bsz, t, BR_W), F32)
    n_alias = len(kv_bufs)
    return pl.pallas_call(
        functools.partial(_proj_kernel, t_minor=t_minor, n_alias=n_alias, nt=nt),
        grid=(n_tiles + 1,),
        in_specs=[seq(D_MODEL), const((D_MODEL, nw)), const((1, nw)),
                  const((2 * 8, D_MODEL)), const((2 * 8, 1))] + hg_specs + rg_specs + ml_specs
                 + [pl.BlockSpec(memory_space=pl.ANY)] * n_alias,
        out_specs=[seq(PA_W), kv_spec, kv_spec, seq(BR_W),
                   pl.BlockSpec((1, BR_W, tt), lambda s: (cur(s)[0], 0, cur(s)[1])),
                   pl.BlockSpec((1, 2 * 8, tt), lambda s: (cur(s)[0], 0, cur(s)[1])),
                   seq_prev(BR_W), s_spec, seq_prev(RG_W), h_spec, buf_spec,
                   seq_prev(BR_W), c_spec, n_spec, m_spec],
        out_shape=[sds(PA_W, F32), kv_sds, kv_sds, sds(BR_W, BF16),
                   jax.ShapeDtypeStruct((bsz, BR_W, t), BF16),
                   jax.ShapeDtypeStruct((bsz, 2 * 8, t), F32),
                   sds(BR_W, BF16), jax.ShapeDtypeStruct((bsz, HG_HEADS, HG_DK, HG_DK), F32),
                   sds(RG_W, BF16), jax.ShapeDtypeStruct((bsz, 1, RG_W), F32),
                   jax.ShapeDtypeStruct((bsz, RG_CONV - 1, RG_W), F32),
                   sds(BR_W, BF16), jax.ShapeDtypeStruct((bsz, ML_HEADS, ML_DK, ML_DV), F32),
                   jax.ShapeDtypeStruct((bsz, ML_HEADS, ML_DK), F32),
                   jax.ShapeDtypeStruct((bsz, 1, ML_HEADS), F32)],
        input_output_aliases={N_PROJ_IN + a: 1 + a for a in range(n_alias)},
        scratch_shapes=[pltpu.VMEM((HG_HEADS, HG_DK, HG_DK), F32), pltpu.VMEM((RG_PAD + tt, RG_W), F32),
                        pltpu.VMEM((ML_HEADS, ML_DV, ML_DK), F32),
                        pltpu.VMEM((2, tt, STASH_W), F32), pltpu.VMEM((2, 2 * 8, tt), F32)],
        compiler_params=_cparams("arbitrary"),
        name="proj",
    )(x3, w, b, wst, bst, *hg, *rg, *ml, *kv_bufs)


def _mlstm_tile(q_ref, k_ref, v_ref, o_ref, sm_ref, st_ref, first, last, c0_ref, n0_ref, m0_ref, g_ref,
                y_ref, c_ref, n_ref, m_ref, ct_ref):
    tt = q_ref.shape[0]
    L = min(tt, ML_CHUNK)
    H = range(ML_HEADS)

    @pl.when(first)
    def _():
        for h in H:
            ct_ref[h] = c0_ref[0, h].T
        n_ref[...] = n0_ref[...]
        m_ref[...] = m0_ref[...]

    tri_l = _tri(L, True)
    tri_u = _tri(L, False)
    causal_t = tri_u > 0.5
    dk = [slice(h * ML_DK, (h + 1) * ML_DK) for h in H]
    dv = [slice(h * ML_DV, (h + 1) * ML_DV) for h in H]
    n_chunk = tt // L
    ctx = [dict() for _ in range(n_chunk)]

    def gates(c):
        k = ctx[c]
        rows = slice(c * L, (c + 1) * L)
        sm = sm_ref[rows, :]
        st = st_ref[:, rows]
        k['i_col'], k['lf_col'] = sm[:, 0:ML_HEADS], _log_sigmoid(sm[:, ML_HEADS:2 * ML_HEADS])
        k['lf_row'] = _log_sigmoid(st[ML_HEADS:2 * ML_HEADS, :])
        k['qf'] = [q_ref[rows, dk[h]] for h in H]
        k['qb'] = [x.astype(BF16) for x in k['qf']]
        k['kf'] = [k_ref[rows, dk[h]] * (ML_DK ** -0.5) for h in H]
        k['vb'] = [v_ref[rows, dv[h]].astype(BF16) for h in H]

    def cumsums(c):
        k = ctx[c]
        k['a_cols'] = k['i_col'] - _tri_dot(tri_l, k['lf_col'])
        k['fcum_row'] = _dot_tri(k['lf_row'], tri_u)
        k['s_t'] = [_dot_nt(k['kf'][h].astype(BF16), k['qb'][h]) for h in H]

    def weights(c):
        k = ctx[c]
        m_all = m_ref[0]
        k['w_t'], k['g'], k['m_t'], k['ks'] = [], [], [], []
        for h in H:
            fr = k['fcum_row'][h:h + 1, :]
            a_col = k['a_cols'][:, h:h + 1]
            d_t = jnp.where(causal_t, fr + a_col, NEG_BIG)
            prev = fr + m_all[:, h:h + 1]
            m_t = jnp.maximum(prev, jnp.max(d_t, 0, keepdims=True))
            k['m_t'].append(m_t)
            k['w_t'].append(jnp.exp(d_t - m_t) * k['s_t'][h])
            k['g'].append(jnp.exp(prev - m_t))
            k['ks'].append(jnp.exp(fr[:, L - 1:L] + a_col - m_t[:, L - 1:L]) * k['kf'][h])
        head = lax.broadcasted_iota(jnp.int32, (1, ML_HEADS), 1)
        m_vec = k['m_t'][0][:, L - 1:L]
        for h in range(1, ML_HEADS):
            m_vec = jnp.where(head == h, k['m_t'][h][:, L - 1:L], m_vec)
        m_ref[0] = m_vec

    def products(c):
        k = ctx[c]
        k['ct'] = [ct_ref[h] for h in H]
        k['nst'] = [n_ref[0, h:h + 1, :] for h in H]
        k['num_t'] = [_dot_tn(k['vb'][h], k['w_t'][h].astype(BF16)) for h in H]
        k['qc_t'] = [_dot_nt(k['ct'][h].astype(BF16), k['qb'][h]) for h in H]
        k['qn'] = [_dot_hi_nt(k['nst'][h], k['qf'][h]) for h in H]
        k['dct'] = [_dot_tn(k['vb'][h], k['ks'][h].astype(BF16)) for h in H]

    def finish_chunk(c):
        k = ctx[c]
        rows = slice(c * L, (c + 1) * L)
        for h in H:
            g_last = k['g'][h][:, L - 1:L]
            ct_ref[h] = g_last * k['ct'][h] + k['dct'][h]
            n_ref[0, h:h + 1, :] = g_last * k['nst'][h] + jnp.sum(k['ks'][h], 0, keepdims=True)
        for h in H:
            g, m_t = k['g'][h], k['m_t'][h]
            den = jnp.sum(k['w_t'][h], 0, keepdims=True) + g * k['qn'][h]
            h_t = (k['num_t'][h] + g * k['qc_t'][h]) / jnp.maximum(jnp.abs(den), jnp.exp(-m_t))
            mu = jnp.mean(h_t, 0, keepdims=True)
            hc = h_t - mu
            var = jnp.mean(hc * hc, 0, keepdims=True)
            hn = (hc * lax.rsqrt(var + 1e-5)).T
            yv = _sigmoid(o_ref[rows, dv[h]]) * (hn * g_ref[:, dv[h]])
            y_ref[0, rows, dv[h]] = yv.astype(y_ref.dtype)

    def finish():
        @pl.when(last)
        def _():
            for h in H:
                c_ref[0, h] = ct_ref[h].T

    stages = (gates, cumsums, weights, products, finish_chunk)
    steps = []
    skew = 2
    for slot in range(len(stages) + skew * (n_chunk - 1)):
        for c in range(n_chunk):
            if 0 <= slot - skew * c < len(stages):
                steps.append(functools.partial(stages[slot - skew * c], c))
    return steps + [finish]


def _fox_kernel(*refs, n_past, tk):
    if n_past:
        (q_ref, sm_ref, st_ref, kb_ref, vt_ref, kp_ref, vp_ref, lfp_ref,
         y_ref, lf_ref, ccol_ref, crun_ref, rrun_ref, yt_ref, qm_ref, m_ref, l_ref) = refs
    else:
        (q_ref, sm_ref, st_ref, kb_ref, vt_ref,
         y_ref, lf_ref, ccol_ref, crun_ref, rrun_ref, yt_ref, qm_ref, m_ref, l_ref) = refs
    tq = q_ref.shape[1]
    tkp = FX_PAST_TILE
    single = kb_ref.shape[1] == tq
    t_idx = 0 if single else pl.program_id(1)
    tri_l = _tri(tq, True)
    tri_u = _tri(tq, False)

    def init():
        crun_ref[...] = jnp.zeros_like(crun_ref)
        if n_past:
            tri_p = _tri(tkp, True)
            for j in range(n_past // tkp):
                cc_p = crun_ref[...] + _tri_dot(tri_p, lfp_ref[0, j * tkp:(j + 1) * tkp, :])
                ccol_ref[j * tkp:(j + 1) * tkp, :] = cc_p * LOG2E
                crun_ref[...] = cc_p[tkp - 1:tkp, :]
        eye = (lax.broadcasted_iota(jnp.int32, (FX_HEADS, FX_HEADS), 0)
               == lax.broadcasted_iota(jnp.int32, (FX_HEADS, FX_HEADS), 1))
        rrun_ref[...] = jnp.sum(jnp.where(eye, crun_ref[...], 0.0), -1, keepdims=True)

    if single:
        init()
    else:
        pl.when(t_idx == 0)(init)

    lf_col = _log_sigmoid(sm_ref[0, :, 2 * ML_HEADS:N_SMALL])
    lf_row = _log_sigmoid(st_ref[0, 2 * ML_HEADS:N_SMALL, :])
    lf_ref[0] = lf_row
    cc = crun_ref[...] + _tri_dot(tri_l, lf_col)
    cr = rrun_ref[...] + _dot_tri(lf_row, tri_u)
    crun_ref[...] = cc[tq - 1:tq, :]
    rrun_ref[...] = cr[:, tq - 1:tq]
    new0 = n_past if single else pl.multiple_of(n_past + t_idx * tq, tq)
    ccol_ref[pl.ds(new0, tq), :] = cc * LOG2E
    cq_rows = cr * LOG2E

    n_sub = tq // tk
    krow = lax.broadcasted_iota(jnp.int32, (tk, tq), 0)
    qcol = lax.broadcasted_iota(jnp.int32, (tk, tq), 1)
    first_head = lax.broadcasted_iota(jnp.int32, (2 * FX_HD, 1), 0) < FX_HD
    qt = (q_ref[0] * (FX_HD ** -0.5 * LOG2E)).T

    for h in range(FX_HEADS):
        qp = qt[(h // 2) * 2 * FX_HD:(h // 2 + 1) * 2 * FX_HD]
        qm_ref[h] = (jnp.where(first_head, qp, 0.0) if h % 2 == 0 else jnp.where(first_head, 0.0, qp)).astype(BF16)
    yt_ref[...] = jnp.zeros_like(yt_ref)
    m_ref[...] = jnp.full(m_ref.shape, NEG_BIG, F32)
    l_ref[...] = jnp.zeros_like(l_ref)

    def tiles(*key_tiles):
        all_scores = []
        for k_pairs, _, _, _, k_is_transposed in key_tiles:
            qk = _dot_tn if k_is_transposed else _dot
            all_scores.append([qk(k_pairs[h // 2], qm_ref[h]) for h in range(FX_HEADS)])
        for (_, vt_pairs, ck, mask, _), ss in zip(key_tiles, all_scores):
            softmax_and_pv(ss, vt_pairs, ck, mask)

    def softmax_and_pv(ss, vt_pairs, ck, mask):
        ps, alphas = [], []
        for h in range(FX_HEADS):
            m_, l_ = m_ref[h:h + 1, :], l_ref[h:h + 1, :]
            u = ss[h] - ck[:, h:h + 1]
            if mask is not None:
                u = jnp.where(mask, u, NEG_BIG)
            cq = cq_rows[h:h + 1, :]
            mn = jnp.maximum(m_, jnp.max(u, 0, keepdims=True) + cq)
            pe = jnp.exp2(u + (cq - mn))
            a = jnp.exp2(m_ - mn)
            m_ref[h:h + 1, :] = mn
            l_ref[h:h + 1, :] = a * l_ + jnp.sum(pe, 0, keepdims=True)
            ps.append(pe.astype(BF16))
            alphas.append(a)
        for h in range(FX_HEADS):
            p, e = divmod(h, 2)
            hrows = slice(h * FX_HD, (h + 1) * FX_HD)
            yt_ref[hrows, :] = alphas[h] * yt_ref[hrows, :] + _dot(vt_pairs[p][e * FX_HD:(e + 1) * FX_HD, :], ps[h])

    pair_lanes = [slice(p * 2 * FX_HD, (p + 1) * 2 * FX_HD) for p in range(FX_HEADS // 2)]
    def past_tile(j):
        k0 = pl.multiple_of(j * tkp, tkp)
        ks = [kp_ref[0, 0, pp, pl.ds(k0, tkp)].astype(BF16) for pp in pair_lanes]
        vs = [vp_ref[0, 0, pp, pl.ds(k0, tkp)].astype(BF16) for pp in pair_lanes]
        return ks, vs, ccol_ref[pl.ds(k0, tkp), :], None, True

    def new_tile(j, mask=None):
        k0 = j * tk if isinstance(j, int) else pl.multiple_of(j * tk, tk)
        ks = [kb_ref[0, pl.ds(k0, tk), pp] for pp in pair_lanes]
        vs = [vt_ref[0, pp, pl.ds(k0, tk)] for pp in pair_lanes]
        return ks, vs, ccol_ref[pl.ds(n_past + k0, tk), :], mask, False

    if n_past:
        n_pt = n_past // tkp
        lax.fori_loop(0, n_pt // 2, lambda j, c: (tiles(past_tile(2 * j), past_tile(2 * j + 1)), c)[1], 0)
        if n_pt % 2:
            tiles(past_tile(n_pt - 1))

    def diag_tiles(first_sub):
        return [new_tile(first_sub + i, krow + i * tk <= qcol) for i in range(n_sub)]

    if single:
        tiles(*diag_tiles(0))
    else:
        n_full = t_idx * n_sub
        lax.fori_loop(0, n_full // 2, lambda j, c: (tiles(new_tile(2 * j), new_tile(2 * j + 1)), c)[1], 0)

        @pl.when(lax.rem(n_full, 2) == 1)
        def _():
            tiles(new_tile(n_full - 1), *diag_tiles(n_full))

        @pl.when(lax.rem(n_full, 2) == 0)
        def _():
            tiles(*diag_tiles(n_full))
    for h in range(FX_HEADS):
        hrows = slice(h * FX_HD, (h + 1) * FX_HD)
        yt_ref[hrows, :] = yt_ref[hrows, :] / l_ref[h:h + 1, :]
    y_ref[0] = yt_ref[...].T.astype(y_ref.dtype)


def _fox(pa3, st, kb3, vt3, past=None, layer=0):
    b, t, _ = pa3.shape
    tq = min(t, 256)
    tk = min(tq, FX_KEY_TILE)
    nt = t // tq
    n_past = 0 if past is None else past[2].shape[1]
    seq = lambda width, blk: pl.BlockSpec((1, tq, width), lambda i, j: (i, j, blk))
    full = lambda n, width: pl.BlockSpec((1, n, width), lambda i, j: (i, 0, 0))
    in_specs = [seq(BR_W, 0), seq(SMALL_W, N_MAIN // SMALL_W),
                pl.BlockSpec((1, 2 * 8, tq), lambda i, j: (i, 0, j)),
                full(t, BR_W), full(BR_W, t)]
    args = [pa3, pa3, st, kb3, vt3]
    assert n_past % FX_PAST_TILE == 0
    if n_past:
        cache = pl.BlockSpec((1, 1, BR_W, n_past), lambda i, j: (layer, i, 0, 0))
        in_specs += [cache, cache, full(n_past, FX_HEADS)]
        args += list(past)
    return pl.pallas_call(
        functools.partial(_fox_kernel, n_past=n_past, tk=tk),
        grid=(b, nt),
        in_specs=in_specs,
        out_specs=[pl.BlockSpec((1, tq, BR_W), lambda i, j: (i, j, 0)),
                   pl.BlockSpec((1, FX_HEADS, tq), lambda i, j: (i, 0, j))],
        out_shape=[jax.ShapeDtypeStruct((b, t, BR_W), BF16),
                   jax.ShapeDtypeStruct((b, FX_HEADS, t), F32)],
        scratch_shapes=[pltpu.VMEM((n_past + t, FX_HEADS), F32),
                        pltpu.VMEM((1, FX_HEADS), F32), pltpu.VMEM((FX_HEADS, 1), F32),
                        pltpu.VMEM((BR_W, tq), F32), pltpu.VMEM((FX_HEADS, 2 * FX_HD, tq), BF16),
                        pltpu.VMEM((FX_HEADS, tq), F32), pltpu.VMEM((FX_HEADS, tq), F32)],
        compiler_params=_cparams("arbitrary", "arbitrary"),
        name="fox",
    )(*args)


def _hgrn_tile(zf, vi, qv_all, gv_all, first, last, lb_ref, gn_ref, s0_ref, y_ref, s_ref, st_ref):
    tt = zf.shape[0]
    L = HG_CHUNK
    B = HG_BLOCK
    assert L == 4 * B

    @pl.when(first)
    def _():
        for h in range(HG_HEADS):
            st_ref[h] = s0_ref[0, h].T

    tri = _tri(L, True)
    row = lax.broadcasted_iota(jnp.int32, (L, 1), 0)
    rb = lax.broadcasted_iota(jnp.int32, (L, L), 0) // B
    cb = lax.broadcasted_iota(jnp.int32, (L, L), 1) // B
    m_diag = _tri(L, True, B) > 0.5
    m_mid = ((rb == 1) & (cb == 0)) | ((rb == 3) & (cb == 2))
    m_far = (rb >= 2) & (cb < 2)
    lb = lb_ref[...]
    heads = [slice(h * HG_DK, (h + 1) * HG_DK) for h in range(HG_HEADS)]
    ctx = [dict() for _ in range(tt // L)]

    def gates(c):
        k = ctx[c]
        z = zf[c * L:(c + 1) * L, :]
        k['logf'] = jnp.log2(jnp.maximum(lb + (1.0 - lb) * _sigmoid(z), TINY))
        k['kk'] = (1.0 - lb) * _sigmoid(-z)
        qv = qv_all[c * L:(c + 1) * L, :]
        k['qq'] = qv * _sigmoid(qv)

    def cumsum(c):
        ctx[c]['b'] = _tri_dot(tri, ctx[c]['logf'])

    def decays(c):
        k = ctx[c]
        b, qq, kk = k['b'], k['qq'], k['kk']
        r1, r2, r3, r4 = (b[j * B - 1:j * B, :] for j in (1, 2, 3, 4))
        start = jnp.where(row < B, 0.0, jnp.where(row < 2 * B, r1, jnp.where(row < 3 * B, r2, r3)))
        mid = jnp.where(row < 2 * B, r1, r3)
        bd = b - start
        e_mid = jnp.exp2(-jnp.abs(b - mid))
        e_far = jnp.exp2(-jnp.abs(b - r2))
        ops = dict(
            qd=qq * jnp.exp2(bd), kd=kk * jnp.exp2(-bd),
            qm=qq * e_mid, km=kk * e_mid, qf=qq * e_far, kf=kk * e_far,
            qs=qq * jnp.exp2(b), ks=kk * jnp.exp2(r4 - b))
        k['ops'] = {name: v.astype(BF16) for name, v in ops.items()}
        k['e4'] = jnp.exp2(r4)
        k['vs'] = [vi[c * L:(c + 1) * L, hl].astype(BF16) for hl in heads]

    def scores(c):
        k = ctx[c]
        ops = k['ops']
        k['a_d'] = [_dot_nt(ops['qd'][:, hl], ops['kd'][:, hl]) for hl in heads]
        k['a_m'] = [_dot_nt(ops['qm'][:, hl], ops['km'][:, hl]) for hl in heads]
        k['a_f'] = [_dot_nt(ops['qf'][:, hl], ops['kf'][:, hl]) for hl in heads]

    def masks(c):
        k = ctx[c]
        k['a'] = [(jnp.where(m_diag, k['a_d'][h], 0.0) + jnp.where(m_mid, k['a_m'][h], 0.0)
                   + jnp.where(m_far, k['a_f'][h], 0.0)).astype(BF16) for h in range(HG_HEADS)]

    def outputs(c):
        k = ctx[c]
        ops, vs = k['ops'], k['vs']
        k['s_old'] = [st_ref[h] for h in range(HG_HEADS)]
        k['o'] = [_dot(k['a'][h], vs[h]) + _dot_nt(ops['qs'][:, heads[h]], k['s_old'][h].astype(BF16))
                  for h in range(HG_HEADS)]
        k['ds'] = [_dot_tn(vs[h], ops['ks'][:, hl]) for h, hl in enumerate(heads)]

    def finish_chunk(c):
        k = ctx[c]
        for h, hl in enumerate(heads):
            st_ref[h] = k['e4'][:, hl] * k['s_old'][h] + k['ds'][h]
        for h, hl in enumerate(heads):
            o = k['o'][h]
            rms = lax.rsqrt(jnp.mean(o * o, -1, keepdims=True) + 1e-6)
            gv = gv_all[c * L:(c + 1) * L, hl]
            y_ref[0, c * L:(c + 1) * L, hl] = (o * rms * gn_ref[:, hl] * (gv * _sigmoid(gv))).astype(y_ref.dtype)

    def finish():
        @pl.when(last)
        def _():
            for h in range(HG_HEADS):
                s_ref[0, h] = st_ref[h].T

    stages = (gates, cumsum, decays, scores, masks, outputs, finish_chunk)
    n_chunk, skew = tt // L, 2
    steps = []
    for slot in range(len(stages) + skew * (n_chunk - 1)):
        for c in range(n_chunk):
            si = slot - skew * c
            if 0 <= si < len(stages):
                steps.append(functools.partial(stages[si], c))
    return steps + [finish]


RG_PAD = 8
RG_PIECE = 64


def _rglru_tile(x, g, first, cw_ref, cb_ref, wa_ref, ba_ref, wx_ref, bx_ref, lam_ref,
                h0_ref, buf0_ref, y_ref, h_ref, buf_ref, xp_ref):
    tt = x.shape[0]
    hist = RG_CONV - 1

    @pl.when(first)
    def _():
        h_ref[...] = h0_ref[...]
        xp_ref[RG_PAD - hist:RG_PAD, :] = buf0_ref[0]

    xp_ref[RG_PAD:RG_PAD + tt, :] = x[...]
    n_piece = max(1, tt // RG_PIECE)
    rows = tt // n_piece
    softplus_neg = jax.nn.softplus(-lam_ref[...])
    sub = lax.broadcasted_iota(jnp.int32, (1, SUBLANE, 1), 1)

    ctx = [dict() for _ in range(n_piece)]

    def conv(i):
        r0 = i * rows
        xr = x[r0:r0 + rows]
        u = cb_ref[...] + xr * cw_ref[hist:hist + 1, :]
        for j in range(hist):
            u = u + xp_ref[RG_PAD - hist + j + r0:RG_PAD - hist + j + r0 + rows, :] * cw_ref[j:j + 1, :]
        ctx[i]['u'] = u

    def gate_dots(i):
        ub = ctx[i]['u'].astype(BF16)
        ctx[i]['ra'] = _dot(ub, wa_ref[...])
        ctx[i]['rx'] = _dot(ub, wx_ref[...])

    def scan(i):
        r0 = i * rows
        u = ctx[i]['u']
        r = _sigmoid(ctx[i]['ra'] + ba_ref[...])
        ig = _sigmoid(ctx[i]['rx'] + bx_ref[...])
        log_a = -RG_C * r * softplus_neg
        a = jnp.exp(log_a)
        bt = jnp.sqrt(jnp.maximum(-jnp.tanh(log_a) * (a * a + 1.0), 0.0)) * (ig * u)
        n_grp = rows // SUBLANE
        a3 = a.reshape(n_grp, SUBLANE, RG_W)
        b3 = bt.reshape(n_grp, SUBLANE, RG_W)
        s = 1
        while s < SUBLANE:
            keep = sub >= s
            a_sh = jnp.where(keep, pltpu.roll(a3, s, 1), 1.0)
            b_sh = jnp.where(keep, pltpu.roll(b3, s, 1), 0.0)
            b3 = a3 * b_sh + b3
            a3 = a3 * a_sh
            s *= 2
        h_in = h_ref[0]
        hs = []
        for gi in range(n_grp):
            hs.append(a3[gi] * h_in + b3[gi])
            h_in = hs[-1][SUBLANE - 1:SUBLANE, :]
        h_ref[0] = h_in
        y_ref[0, r0:r0 + rows, :] = (jnp.concatenate(hs, axis=0) * _gelu(g[r0:r0 + rows])).astype(y_ref.dtype)

    def finish():
        new_buf = xp_ref[RG_PAD + tt - hist:RG_PAD + tt, :]
        xp_ref[RG_PAD - hist:RG_PAD, :] = new_buf
        buf_ref[0] = new_buf

    stages = (conv, gate_dots, scan)
    steps = []
    for slot in range(len(stages) + n_piece - 1):
        for i in range(n_piece):
            if 0 <= slot - i < len(stages):
                steps.append(functools.partial(stages[slot - i], i))
    return steps + [finish]


def _merge_kernel(x_ref, y0_ref, y1_ref, y2_ref, y3_ref, wmg_ref, bmg_ref, wbr_ref, wout_ref,
                  g_ref, b_ref, o_ref, *, alpha):
    tm = x_ref.shape[0]
    n_part = 2 if tm % 16 == 0 else 1
    parts = [slice(i * tm // n_part, (i + 1) * tm // n_part) for i in range(n_part)]

    def branches(rows):
        xb = x_ref[rows, :].astype(BF16)
        mix = None
        for m, y_ref in enumerate((y0_ref, y1_ref, y2_ref, y3_ref)):
            gate = _sigmoid(_dot(xb, wmg_ref[m]) + bmg_ref[m])
            term = gate * _dot(y_ref[rows, :], wbr_ref[m])
            mix = term if mix is None else mix + term
        return mix

    mixes = [branches(rows) for rows in parts]
    for rows, mix in zip(parts, mixes):
        o_ref[rows, :] = _layer_norm(alpha * x_ref[rows, :] + _dot(mix.astype(BF16), wout_ref[...]),
                                     g_ref[...], b_ref[...])


def _merge(x2d, ys, wmg, bmg, wbr, wout, g, bb, alpha):
    m = x2d.shape[0]
    tm = min(m, DENSE_ROWS)
    row = lambda width: pl.BlockSpec((tm, width), lambda i: (i, 0))
    c2 = lambda shape: pl.BlockSpec(shape, lambda i: (0, 0))
    c3 = lambda shape: pl.BlockSpec(shape, lambda i: (0, 0, 0))
    return pl.pallas_call(
        functools.partial(_merge_kernel, alpha=alpha),
        grid=(m // tm,),
        in_specs=[row(D_MODEL)] + [row(BR_W)] * N_BRANCH +
                 [c3((N_BRANCH, D_MODEL, D_MODEL)), c3((N_BRANCH, 1, D_MODEL)),
                  c3((N_BRANCH, BR_W, D_MODEL)), c2((D_MODEL, D_MODEL)),
                  c2((1, D_MODEL)), c2((1, D_MODEL))],
        out_specs=row(D_MODEL),
        out_shape=jax.ShapeDtypeStruct((m, D_MODEL), F32),
        compiler_params=_cparams("arbitrary"),
        name="merge",
    )(x2d, *ys, wmg, bmg, wbr, wout, g, bb)


FF_PAD = 8
FF_COL_BLK = 512


def _ffn_kernel(x_ref, wg_ref, wu_ref, cw_ref, cb_ref, wd_ref, g_ref, b_ref, buf0_ref,
                o_ref, buf_ref, gp_ref, *, alpha):
    tt = x_ref.shape[1]
    hist = FFN_CONV - 1

    @pl.when(pl.program_id(1) == 0)
    def _():
        gp_ref[FF_PAD - hist:FF_PAD, :] = buf0_ref[0]

    x = x_ref[0]
    xb = x.astype(BF16)
    chunks = [slice(c0, c0 + FF_COL_BLK) for c0 in range(0, D_FF, FF_COL_BLK)]

    def in_proj(cols):
        gp = _dot(xb, wg_ref[:, cols])
        gp_ref[FF_PAD:FF_PAD + tt, cols] = gp
        return gp, _dot(xb, wu_ref[:, cols])

    def out_proj(cols, gp, up):
        gc = cb_ref[:, cols] + gp * cw_ref[hist:hist + 1, cols]
        for j in range(hist):
            gc = gc + gp_ref[FF_PAD - hist + j:FF_PAD - hist + j + tt, cols] * cw_ref[j:j + 1, cols]
        return _dot((_gelu(gc) * up).astype(BF16), wd_ref[cols, :])

    acc = None
    pending = in_proj(chunks[0])
    for i, cols in enumerate(chunks):
        nxt = in_proj(chunks[i + 1]) if i + 1 < len(chunks) else None
        part = out_proj(cols, *pending)
        acc = part if acc is None else acc + part
        pending = nxt
    new_buf = gp_ref[FF_PAD + tt - hist:FF_PAD + tt, :]
    gp_ref[FF_PAD - hist:FF_PAD, :] = new_buf
    buf_ref[0] = new_buf
    o_ref[0] = _layer_norm(alpha * x + acc, g_ref[...], b_ref[...])


def _ffn(x3, wg, wu, cw, cb, wd, g, bb, buf0, alpha):
    b, t, _ = x3.shape
    tt = min(t, DENSE_ROWS)
    nt = t // tt
    c2 = lambda shape: pl.BlockSpec(shape, lambda i, j: (0, 0))
    buf_spec = pl.BlockSpec((1, FFN_CONV - 1, D_FF), lambda i, j: (i, 0, 0))
    seq = pl.BlockSpec((1, tt, D_MODEL), lambda i, j: (i, j, 0))
    return pl.pallas_call(
        functools.partial(_ffn_kernel, alpha=alpha),
        grid=(b, nt),
        in_specs=[seq, c2((D_MODEL, D_FF)), c2((D_MODEL, D_FF)), c2((FFN_CONV, D_FF)), c2((1, D_FF)),
                  c2((D_FF, D_MODEL)), c2((1, D_MODEL)), c2((1, D_MODEL)), buf_spec],
        out_specs=[seq, buf_spec],
        out_shape=[jax.ShapeDtypeStruct(x3.shape, F32), jax.ShapeDtypeStruct(buf0.shape, F32)],
        scratch_shapes=[pltpu.VMEM((FF_PAD + tt, D_FF), F32)],
        compiler_params=_cparams("arbitrary", "arbitrary"),
        name="ffn",
    )(x3, wg, wu, cw, cb, wd, g, bb, buf0)


def _prep_layer(l, w_in, b_in, ml_norm_g, hg_norm_g, lbs, rg_conv_w, rg_conv_b, rg_w_a, rg_b_a,
                rg_w_x, rg_b_x, rg_lambda, w_mg, b_mg, w_br, w_out, ln1_g, ln1_b,
                w_ff_gate, w_ff_up, ff_conv_w, ff_conv_b, w_ff_down, ln2_g, ln2_b):
    offs = [int(o) for o in np.concatenate([[0], np.cumsum(PROJ_SIZES)])]
    wl, bl = w_in[l].astype(BF16), b_in[l]
    cols = lambda a, segs: [a[..., offs[s]:offs[s + 1]] for s in segs]
    w_small = jnp.concatenate(cols(wl, SMALL_SEGS), axis=1)
    b_small = jnp.concatenate(cols(bl, SMALL_SEGS))
    w = jnp.concatenate(cols(wl, MAIN_SEGS) + [w_small, jnp.zeros((D_MODEL, SMALL_W - N_SMALL), BF16)]
                        + cols(wl, STASH_SEGS) + cols(wl, KV_SEGS), axis=1)
    bvec = jnp.concatenate(cols(bl, MAIN_SEGS) + [b_small, jnp.zeros((SMALL_W - N_SMALL,), F32)]
                           + cols(bl, STASH_SEGS) + cols(bl, KV_SEGS))[None, :]
    wst = w_small.T
    bst = b_small[:, None]

    def block_diag(wb):
        eye = jnp.eye(RG_BLOCKS, dtype=F32)
        return jnp.einsum('nde,nm->ndme', wb, eye).reshape(RG_W, RG_W).astype(BF16)

    row = lambda v: v[None, :].astype(F32)
    return dict(
        w=w, b=bvec, wst=wst, bst=bst, ml_g=row(ml_norm_g[l]), hg_g=row(hg_norm_g[l]), lb=row(lbs[l]),
        rg_cw=rg_conv_w[l], rg_cb=row(rg_conv_b[l]), rg_wa=block_diag(rg_w_a[l]), rg_ba=row(rg_b_a[l]),
        rg_wx=block_diag(rg_w_x[l]), rg_bx=row(rg_b_x[l]), rg_lam=row(rg_lambda[l]),
        wmg=w_mg[l].astype(BF16), bmg=b_mg[l][:, None, :], wbr=w_br[l].astype(BF16),
        wout=w_out[l].astype(BF16), ln1_g=row(ln1_g[l]), ln1_b=row(ln1_b[l]),
        wg=w_ff_gate[l].astype(BF16), wu=w_ff_up[l].astype(BF16), ff_cw=ff_conv_w[l],
        ff_cb=row(ff_conv_b[l]), wd=w_ff_down[l].astype(BF16), ln2_g=row(ln2_g[l]), ln2_b=row(ln2_b[l]))


def _kv_t_minor(t):
    return t >= LANE


def _trunk_layer(x, p, fox_past, ml_c, ml_n, ml_m, hg_s, rg_h, rg_buf, ff_buf, alpha, layer, depth, kv_bufs):
    b, t, _ = x.shape
    assert t % 64 == 0 and (t <= 256 or t % 256 == 0)
    x2d = x.reshape(b * t, D_MODEL)
    rg = (p['rg_cw'], p['rg_cb'], p['rg_wa'], p['rg_ba'], p['rg_wx'], p['rg_bx'], p['rg_lam'],
          rg_h.reshape(b, 1, RG_W), rg_buf)
    hg = (p['lb'], p['hg_g'], hg_s)
    ml = (ml_c, ml_n, ml_m.reshape(b, 1, ML_HEADS), p['ml_g'])
    (pa3, k32, v32, kb, vt, st, y_hg, hg_s, y_rg, rg_h, rg_buf, y_ml, ml_c, ml_n, ml_m) = _proj(
        x, p['w'], p['b'], p['wst'], p['bst'], hg, rg, ml, layer, depth, _kv_t_minor(t), kv_bufs)
    past = None
    if fox_past is not None:
        k_cache, v_cache, logf_past = fox_past
        to_feature_major = lambda a: jnp.transpose(a, (0, 1, 3, 4, 2)).reshape(depth, b, BR_W, a.shape[2])
        past = (to_feature_major(k_cache), to_feature_major(v_cache), logf_past)
    y_fx, f_log = _fox(pa3, st, kb, vt, past, layer)
    f_log = jnp.swapaxes(f_log, 1, 2)
    ys = [y.reshape(b * t, BR_W) for y in (y_ml, y_fx, y_hg, y_rg)]
    x1 = _merge(x2d, ys, p['wmg'], p['bmg'], p['wbr'], p['wout'], p['ln1_g'], p['ln1_b'], alpha)
    x2, ff_buf = _ffn(x1.reshape(b, t, D_MODEL), p['wg'], p['wu'], p['ff_cw'], p['ff_cb'], p['wd'],
                      p['ln2_g'], p['ln2_b'], ff_buf, alpha)
    state = (f_log, ml_c, ml_n, ml_m.reshape(b, ML_HEADS), hg_s, rg_h.reshape(b, RG_W), rg_buf, ff_buf)
    return x2, (k32, v32), state


def _kv_output(buf, t):
    depth, b = buf.shape[:2]
    if _kv_t_minor(t):
        return jnp.transpose(buf.reshape(depth, b, FX_HEADS, FX_HD, t), (0, 1, 4, 2, 3))
    return buf.reshape(depth, b, t, FX_HEADS, FX_HD)


def kernel(x_prompt, x_sample, cache_fox_k, cache_fox_v, cache_fox_logf, state_mlstm_c, state_mlstm_n,
           state_mlstm_m, state_hgrn_s, state_rglru_h, state_rglru_conv, state_ffn_conv,
           w_in, b_in, ml_norm_g, hg_norm_g, hg_lb_logits, rg_conv_w, rg_conv_b, rg_w_a, rg_b_a,
           rg_w_x, rg_b_x, rg_lambda, w_mg, b_mg, w_br, w_out, ln1_g, ln1_b,
           w_ff_gate, w_ff_up, ff_conv_w, ff_conv_b, w_ff_down, ln2_g, ln2_b):
    depth = w_in.shape[0]
    alpha = (2 * depth) ** 0.25
    pl_soft = jax.nn.softmax(hg_lb_logits.astype(F32), axis=0)
    lbs = jnp.cumsum(pl_soft, axis=0) - pl_soft[0]
    bp = x_prompt.shape[0]
    yp, ys = x_prompt, x_sample
    p_new, s_new = [], []
    kv_p, kv_s = (), ()
    for l in range(depth):
        p = _prep_layer(l, w_in, b_in, ml_norm_g, hg_norm_g, lbs, rg_conv_w, rg_conv_b, rg_w_a, rg_b_a,
                        rg_w_x, rg_b_x, rg_lambda, w_mg, b_mg, w_br, w_out, ln1_g, ln1_b,
                        w_ff_gate, w_ff_up, ff_conv_w, ff_conv_b, w_ff_down, ln2_g, ln2_b)
        z = lambda *shape: jnp.zeros((bp,) + shape, F32)
        yp, kv_p, st_p = _trunk_layer(yp, p, None, z(ML_HEADS, ML_DK, ML_DV), z(ML_HEADS, ML_DK), z(ML_HEADS),
                                      z(HG_HEADS, HG_DK, HG_DK), z(RG_W), z(RG_CONV - 1, RG_W),
                                      z(FFN_CONV - 1, D_FF), alpha, l, depth, kv_p if l else ())
        p_new.append(st_p)
        ys, kv_s, st_s = _trunk_layer(ys, p, (cache_fox_k, cache_fox_v, cache_fox_logf[l]),
                                      state_mlstm_c[l], state_mlstm_n[l], state_mlstm_m[l], state_hgrn_s[l],
                                      state_rglru_h[l], state_rglru_conv[l], state_ffn_conv[l], alpha,
                                      l, depth, kv_s if l else ())
        s_new.append(st_s)
    n_st = len(p_new[0])
    p_out = [jnp.stack([st[j] for st in p_new]) for j in range(n_st)]
    s_out = [jnp.stack([st[j] for st in s_new]) for j in range(n_st)]
    tp, ts = x_prompt.shape[1], x_sample.shape[1]
    return (yp, ys, _kv_output(kv_p[0], tp), _kv_output(kv_p[1], tp), *p_out,
            _kv_output(kv_s[0], ts), _kv_output(kv_s[1], ts), *s_out)
```
